```python
import jax, jax.numpy as jnp
from jax import lax
import numpy as np

D_MODEL = 1024
BATCH = 4
SEQ = 4096
DEPTH = 2
DEC_BATCH = 128
DEC_SEQ = 1
PAST_LEN = 2048
PAGE_SIZE = 128

HEAD_DIM = 64
NSA_HEADS = 8
NSA_KV_GROUPS = 2
NSA_REP = NSA_HEADS // NSA_KV_GROUPS
MOBA_HEADS = 8
NSA_WIDTH = NSA_HEADS * HEAD_DIM
NSA_KV_WIDTH = NSA_KV_GROUPS * HEAD_DIM
MOBA_WIDTH = MOBA_HEADS * HEAD_DIM
CMP_LEN = 32
CMP_STRIDE = 16
CMP_HIDDEN = 2 * HEAD_DIM
SEL_BLOCK = 64
SEL_TOPK = 16
WINDOW = 512
MOBA_BLOCK = 256
MOBA_TOPK = 3
D_FF = 4 * D_MODEL
Q_CHUNK = 32
ROPE_THETA = 10000.0
NORM_EPS = 1e-6
NEG = -1e30
FORCED = 1e6
SCALE = HEAD_DIM ** -0.5
PAD_MULT = MOBA_BLOCK
SPLIT_SIZES = (NSA_WIDTH, NSA_KV_WIDTH, NSA_KV_WIDTH, NSA_KV_WIDTH, NSA_KV_WIDTH, NSA_KV_WIDTH, NSA_KV_WIDTH,
               3 * NSA_HEADS, MOBA_WIDTH, MOBA_WIDTH, MOBA_WIDTH, 2 * D_MODEL)
IN_WIDTH = NSA_WIDTH + 6 * NSA_KV_WIDTH + 3 * NSA_HEADS + 3 * MOBA_WIDTH + 2 * D_MODEL

kernel_name = "nsa_moba_gated_hybrid_step"


def rmsnorm(x, g):
    xf = x.astype(jnp.float32)
    y = xf * lax.rsqrt(jnp.mean(xf * xf, axis=-1, keepdims=True) + NORM_EPS)
    return (y * g.astype(jnp.float32)).astype(x.dtype)


def rope(x, pos):
    half = HEAD_DIM // 2
    inv = ROPE_THETA ** (-jnp.arange(half, dtype=jnp.float32) / half)
    ang = pos.astype(jnp.float32)[:, None] * inv[None, :]
    cos, sin = jnp.cos(ang)[:, None, :], jnp.sin(ang)[:, None, :]
    xf = x.astype(jnp.float32)
    x1, x2 = xf[..., :half], xf[..., half:]
    return jnp.concatenate([x1 * cos - x2 * sin, x2 * cos + x1 * sin], axis=-1).astype(x.dtype)


def masked_softmax(s, mask):
    s = jnp.where(mask, s.astype(jnp.float32), NEG)
    return jax.nn.softmax(s, axis=-1) * mask


def pad_seq(a, mult):
    extra = (-a.shape[1]) % mult
    return jnp.pad(a, [(0, 0), (0, extra)] + [(0, 0)] * (a.ndim - 2))


def to_blocks(a, blk):
    b, l, h, d = a.shape
    return a.reshape(b, l // blk, blk, h, d).transpose(0, 3, 1, 2, 4)


def project(h, pos, w_in, gq_nsa, gk_sel, gk_win, gq_moba, gk_moba):
    b, t = h.shape[:2]
    parts = jnp.split(h @ w_in, np.cumsum(SPLIT_SIZES)[:-1].tolist(), axis=-1)
    qa, kc, vc, ks, vs, kw, vw, ga, qb, kb, vb, gm = parts
    heads = lambda a, n: a.reshape(b, t, n, HEAD_DIM)
    qn = rmsnorm(heads(qa, NSA_HEADS), gq_nsa)
    qr = rope(qn, pos)
    kc, vc = heads(kc, NSA_KV_GROUPS), heads(vc, NSA_KV_GROUPS)
    ks = rope(rmsnorm(heads(ks, NSA_KV_GROUPS), gk_sel), pos)
    vs = heads(vs, NSA_KV_GROUPS)
    kw = rope(rmsnorm(heads(kw, NSA_KV_GROUPS), gk_win), pos)
    vw = heads(vw, NSA_KV_GROUPS)
    ga = jax.nn.sigmoid(ga).reshape(b, t, 3, NSA_HEADS)
    qb = rope(rmsnorm(heads(qb, MOBA_HEADS), gq_moba), pos)
    kb = rope(rmsnorm(heads(kb, MOBA_HEADS), gk_moba), pos)
    vb = heads(vb, MOBA_HEADS)
    gm = jax.nn.sigmoid(gm).reshape(b, t, 2, D_MODEL)
    return qn, qr, kc, vc, ks, vs, kw, vw, ga, qb, kb, vb, gm


def compress_rows(rows, pos_emb, w1, w2):
    b, l, g, d = rows.shape
    sub = rows.reshape(b, l // CMP_STRIDE, CMP_STRIDE, g, d)
    blk = jnp.concatenate([sub[:, :-1], sub[:, 1:]], axis=2) + pos_emb[None, None, :, None, :].astype(rows.dtype)
    flat = blk.transpose(0, 1, 3, 2, 4).reshape(b, l // CMP_STRIDE - 1, g, CMP_LEN * d)
    return jax.nn.gelu(flat @ w1) @ w2


def build_keys(kc, vc, ks, vs, kb, vb, cmp_pos_k, cmp_pos_v, w1k, w2k, w1v, w2v, gk_cmp):
    kc, vc, ks, vs, kb, vb = (pad_seq(a, PAD_MULT) for a in (kc, vc, ks, vs, kb, vb))
    k_c = rmsnorm(compress_rows(kc, cmp_pos_k, w1k, w2k), gk_cmp)
    v_c = compress_rows(vc, cmp_pos_v, w1v, w2v)
    c_end = jnp.arange(k_c.shape[1], dtype=jnp.int32) * CMP_STRIDE + (CMP_LEN - 1)
    kb_t, vb_t = to_blocks(kb, MOBA_BLOCK), to_blocks(vb, MOBA_BLOCK)
    kmean = jnp.mean(kb_t.astype(jnp.float32), axis=3).astype(kb.dtype).transpose(0, 2, 1, 3)
    return k_c, v_c, c_end, to_blocks(ks, SEL_BLOCK), to_blocks(vs, SEL_BLOCK), kb_t, vb_t, kmean


def nsa_attend(qn, qr, ga, t_pos, k_c, v_c, c_end, ks_blk, vs_blk, kw, vw, kw_pos):
    b, q = qn.shape[:2]
    G, R = NSA_KV_GROUPS, NSA_REP
    qn = qn.reshape(b, q, G, R, HEAD_DIM)
    qr = qr.reshape(b, q, G, R, HEAD_DIM)
    s = jnp.einsum('bqgrd,bngd->bqgrn', qn, k_c) * SCALE
    p_c = masked_softmax(s, (c_end[None, :] <= t_pos[:, None])[None, :, None, None, :])
    o_c = jnp.einsum('bqgrn,bngd->bqgrd', p_c.astype(v_c.dtype), v_c)
    ns = ks_blk.shape[2]
    p_sub = jnp.pad(p_c, [(0, 0)] * 4 + [(0, 1)]) + jnp.pad(p_c, [(0, 0)] * 4 + [(1, 0)])
    imp = p_sub.reshape(b, q, G, R, ns, SEL_BLOCK // CMP_STRIDE).sum(axis=(3, 5))
    blk_ids = jnp.arange(ns, dtype=jnp.int32)[None, :]
    cur = (t_pos // SEL_BLOCK)[:, None]
    forced = (blk_ids == 0) | (blk_ids >= cur - 1)
    score = jnp.where((blk_ids > cur)[None, :, None, :], NEG, jnp.where(forced[None, :, None, :], FORCED, imp))
    _, idx = lax.top_k(score, min(SEL_TOPK, ns))
    n = idx.shape[-1]
    valid = idx <= (t_pos // SEL_BLOCK)[None, :, None, None]
    bi = jnp.arange(b)[:, None, None, None]
    gi = jnp.arange(G)[None, None, :, None]
    k_s = ks_blk[bi, gi, idx]
    v_s = vs_blk[bi, gi, idx].reshape(b, q, G, n * SEL_BLOCK, HEAD_DIM)
    s = jnp.einsum('bqgrd,bqgnsd->bqgrns', qr, k_s).reshape(b, q, G, R, n * SEL_BLOCK) * SCALE
    kpos = idx[..., None] * SEL_BLOCK + jnp.arange(SEL_BLOCK, dtype=jnp.int32)
    m_s = (valid[..., None] & (kpos <= t_pos[None, :, None, None, None])).reshape(b, q, G, 1, n * SEL_BLOCK)
    p_s = masked_softmax(s, m_s)
    o_s = jnp.einsum('bqgrk,bqgkd->bqgrd', p_s.astype(v_s.dtype), v_s)
    s = jnp.einsum('bqgrd,bkgd->bqgrk', qr, kw) * SCALE
    rel = t_pos[:, None] - kw_pos[None, :]
    m_w = ((rel >= 0) & (rel < WINDOW) & (kw_pos[None, :] >= 0))[None, :, None, None, :]
    p_w = masked_softmax(s, m_w)
    o_w = jnp.einsum('bqgrk,bkgd->bqgrd', p_w.astype(vw.dtype), vw)
    g = ga.reshape(b, q, 3, G, R, 1)
    o = g[:, :, 0] * o_c + g[:, :, 1] * o_s + g[:, :, 2] * o_w
    return o.reshape(b, q, NSA_WIDTH)


def moba_attend(q, t_pos, kb_t, vb_t, kmean):
    b, nq = q.shape[:2]
    nb = kb_t.shape[2]
    cq = t_pos // MOBA_BLOCK
    s_blk = jnp.einsum('bqhd,bnhd->bqhn', q, kmean).astype(jnp.float32)
    past = (jnp.arange(nb, dtype=jnp.int32)[None, :] < cq[:, None])[None, :, None, :]
    _, idx = lax.top_k(jnp.where(past, s_blk, NEG), min(MOBA_TOPK, nb))
    n = idx.shape[-1]
    valid = idx < cq[None, :, None, None]
    bi = jnp.arange(b)[:, None, None, None]
    hi = jnp.arange(MOBA_HEADS)[None, None, :, None]
    k_s = kb_t[bi, hi, idx].reshape(b, nq, MOBA_HEADS, n * MOBA_BLOCK, HEAD_DIM)
    v_s = vb_t[bi, hi, idx].reshape(b, nq, MOBA_HEADS, n * MOBA_BLOCK, HEAD_DIM)
    k_o, v_o = kb_t[:, :, cq], vb_t[:, :, cq]
    s_sel = jnp.einsum('bqhd,bqhkd->bqhk', q, k_s)
    s_own = jnp.einsum('bqhd,bhqsd->bqhs', q, k_o)
    own_ok = (cq[:, None] * MOBA_BLOCK + jnp.arange(MOBA_BLOCK, dtype=jnp.int32)[None, :] <= t_pos[:, None])[None, :, None, :]
    mask = jnp.concatenate([jnp.repeat(valid, MOBA_BLOCK, axis=-1),
                            jnp.broadcast_to(own_ok, (b, nq, MOBA_HEADS, MOBA_BLOCK))], axis=-1)
    p = masked_softmax(jnp.concatenate([s_sel, s_own], axis=-1) * SCALE, mask).astype(v_s.dtype)
    o = (jnp.einsum('bqhk,bqhkd->bqhd', p[..., :n * MOBA_BLOCK], v_s)
         + jnp.einsum('bqhs,bhqsd->bqhd', p[..., n * MOBA_BLOCK:], v_o))
    return o.reshape(b, nq, MOBA_WIDTH)


def prompt_mixers(qn, qr, ga, qb, kw, vw, keys):
    k_c, v_c, c_end, ks_blk, vs_blk, kb_t, vb_t, kmean = keys
    b, t = qn.shape[:2]
    kw_pad = jnp.pad(kw, ((0, 0), (WINDOW, 0), (0, 0), (0, 0)))
    vw_pad = jnp.pad(vw, ((0, 0), (WINDOW, 0), (0, 0), (0, 0)))

    def chunk(i):
        st = i * Q_CHUNK
        t_pos = st + jnp.arange(Q_CHUNK, dtype=jnp.int32)
        take = lambda a: lax.dynamic_slice_in_dim(a, st, Q_CHUNK, axis=1)
        band = lambda a: lax.dynamic_slice_in_dim(a, st, WINDOW + Q_CHUNK, axis=1)
        kw_pos = st - WINDOW + jnp.arange(WINDOW + Q_CHUNK, dtype=jnp.int32)
        oa = nsa_attend(take(qn), take(qr), take(ga), t_pos, k_c, v_c, c_end, ks_blk, vs_blk,
                        band(kw_pad), band(vw_pad), kw_pos)
        ob = moba_attend(take(qb), t_pos, kb_t, vb_t, kmean)
        return oa, ob

    oa, ob = lax.map(chunk, jnp.arange(t // Q_CHUNK, dtype=jnp.int32))
    unchunk = lambda o: o.transpose(1, 0, 2, 3).reshape(b, t, o.shape[-1])
    return unchunk(oa), unchunk(ob)


def merge_branches(x, oa, ob, gm, w_up_nsa, w_up_moba, w_out):
    mixed = gm[:, :, 0] * (oa @ w_up_nsa) + gm[:, :, 1] * (ob @ w_up_moba)
    return x + mixed @ w_out


def sq_relu_mlp(x, g, w_up, w_down):
    return x + jnp.square(jax.nn.relu(rmsnorm(x, g) @ w_up)) @ w_down


def setup_inputs(seed: int = 0) -> dict:
    key = jax.random.key(seed)
    k = jax.random.split(key, 32)
    f32 = jnp.float32
    nrm = lambda kk, shape, scale: jax.random.normal(kk, shape, f32) * scale
    gain = lambda kk, shape: 1.0 + nrm(kk, shape, 0.01)
    n_pages = PAST_LEN // PAGE_SIZE
    n_used = DEC_BATCH * n_pages
    n_pool = n_used + max(1, n_used // 4)
    w_buf = min(WINDOW, PAST_LEN)
    page_table = jax.random.permutation(k[5], n_pool)[:n_used].reshape(DEC_BATCH, n_pages).astype(jnp.int32)
    return {
        'x_prompt': nrm(k[0], (BATCH, SEQ, D_MODEL), 1.0),
        'x_sample': nrm(k[1], (DEC_BATCH, DEC_SEQ, D_MODEL), 1.0),
        'cache_nsa_cmp': nrm(k[2], (DEPTH, n_pool, PAGE_SIZE, 2, NSA_KV_GROUPS, HEAD_DIM), 1.0),
        'cache_nsa_sel': nrm(k[3], (DEPTH, n_pool, PAGE_SIZE, 2, NSA_KV_GROUPS, HEAD_DIM), 1.0),
        'cache_moba': nrm(k[4], (DEPTH, n_pool, PAGE_SIZE, 2, MOBA_HEADS, HEAD_DIM), 1.0),
        'state_nsa_win': nrm(k[6], (DEPTH, DEC_BATCH, w_buf, 2, NSA_KV_GROUPS, HEAD_DIM), 1.0),
        'page_table': page_table,
        'norm_mix': gain(k[7], (DEPTH, D_MODEL)),
        'w_in': nrm(k[8], (DEPTH, D_MODEL, IN_WIDTH), D_MODEL ** -0.5),
        'gq_nsa': gain(k[9], (DEPTH, HEAD_DIM)),
        'gk_cmp': gain(k[10], (DEPTH, HEAD_DIM)),
        'gk_sel': gain(k[11], (DEPTH, HEAD_DIM)),
        'gk_win': gain(k[12], (DEPTH, HEAD_DIM)),
        'gq_moba': gain(k[13], (DEPTH, HEAD_DIM)),
        'gk_moba': gain(k[14], (DEPTH, HEAD_DIM)),
        'cmp_pos_k': nrm(k[15], (DEPTH, CMP_LEN, HEAD_DIM), 0.1),
        'cmp_pos_v': nrm(k[16], (DEPTH, CMP_LEN, HEAD_DIM), 0.1),
        'cmp_w1_k': nrm(k[17], (DEPTH, CMP_LEN * HEAD_DIM, CMP_HIDDEN), (CMP_LEN * HEAD_DIM) ** -0.5),
        'cmp_w2_k': nrm(k[18], (DEPTH, CMP_HIDDEN, HEAD_DIM), CMP_HIDDEN ** -0.5),
        'cmp_w1_v': nrm(k[19], (DEPTH, CMP_LEN * HEAD_DIM, CMP_HIDDEN), (CMP_LEN * HEAD_DIM) ** -0.5),
        'cmp_w2_v': nrm(k[20], (DEPTH, CMP_HIDDEN, HEAD_DIM), CMP_HIDDEN ** -0.5),
        'w_up_nsa': nrm(k[21], (DEPTH, NSA_WIDTH, D_MODEL), NSA_WIDTH ** -0.5),
        'w_up_moba': nrm(k[22], (DEPTH, MOBA_WIDTH, D_MODEL), MOBA_WIDTH ** -0.5),
        'w_out': nrm(k[23], (DEPTH, D_MODEL, D_MODEL), D_MODEL ** -0.5),
        'norm_mlp': gain(k[24], (DEPTH, D_MODEL)),
        'w_mlp_up': nrm(k[25], (DEPTH, D_MODEL, D_FF), D_MODEL ** -0.5),
        'w_mlp_down': nrm(k[26], (DEPTH, D_FF, D_MODEL), D_FF ** -0.5),
    }


def reference(x_prompt, x_sample, cache_nsa_cmp, cache_nsa_sel, cache_moba, state_nsa_win, page_table,
              norm_mix, w_in, gq_nsa, gk_cmp, gk_sel, gk_win, gq_moba, gk_moba,
              cmp_pos_k, cmp_pos_v, cmp_w1_k, cmp_w2_k, cmp_w1_v, cmp_w2_v,
              w_up_nsa, w_up_moba, w_out, norm_mlp, w_mlp_up, w_mlp_down):
    xp, xs = x_prompt, x_sample
    n_dec, n_new = xs.shape[0], xs.shape[1]
    past_len = page_table.shape[1] * PAGE_SIZE
    pos_p = jnp.arange(xp.shape[1], dtype=jnp.int32)
    pos_s = past_len + jnp.arange(n_new, dtype=jnp.int32)
    p_cmp, p_sel, p_moba, p_win = [], [], [], []
    s_cmp, s_sel, s_moba, s_win = [], [], [], []
    for l in range(DEPTH):
        cmp_params = (cmp_pos_k[l], cmp_pos_v[l], cmp_w1_k[l], cmp_w2_k[l], cmp_w1_v[l], cmp_w2_v[l], gk_cmp[l])
        qn, qr, kc, vc, ks, vs, kw, vw, ga, qb, kb, vb, gm = project(
            rmsnorm(xp, norm_mix[l]), pos_p, w_in[l], gq_nsa[l], gk_sel[l], gk_win[l], gq_moba[l], gk_moba[l])
        keys = build_keys(kc, vc, ks, vs, kb, vb, *cmp_params)
        oa, ob = prompt_mixers(qn, qr, ga, qb, kw, vw, keys)
        xp = merge_branches(xp, oa, ob, gm, w_up_nsa[l], w_up_moba[l], w_out[l])
        xp = sq_relu_mlp(xp, norm_mlp[l], w_mlp_up[l], w_mlp_down[l])
        p_cmp.append(jnp.stack([kc, vc], axis=2))
        p_sel.append(jnp.stack([ks, vs], axis=2))
        p_moba.append(jnp.stack([kb, vb], axis=2))
        p_win.append(jnp.stack([kw, vw], axis=2)[:, -min(WINDOW, xp.shape[1]):])
        qn, qr, kc, vc, ks, vs, kw, vw, ga, qb, kb, vb, gm = project(
            rmsnorm(xs, norm_mix[l]), pos_s, w_in[l], gq_nsa[l], gk_sel[l], gk_win[l], gq_moba[l], gk_moba[l])
        new_cmp = jnp.stack([kc, vc], axis=2)
        new_sel = jnp.stack([ks, vs], axis=2)
        new_moba = jnp.stack([kb, vb], axis=2)
        all_cmp = jnp.concatenate([cache_nsa_cmp[l][page_table].reshape(n_dec, past_len, 2, NSA_KV_GROUPS, HEAD_DIM), new_cmp], axis=1)
        all_sel = jnp.concatenate([cache_nsa_sel[l][page_table].reshape(n_dec, past_len, 2, NSA_KV_GROUPS, HEAD_DIM), new_sel], axis=1)
        all_moba = jnp.concatenate([cache_moba[l][page_table].reshape(n_dec, past_len, 2, MOBA_HEADS, HEAD_DIM), new_moba], axis=1)
        k_c, v_c, c_end, ks_blk, vs_blk, kb_t, vb_t, kmean = build_keys(
            all_cmp[:, :, 0], all_cmp[:, :, 1], all_sel[:, :, 0], all_sel[:, :, 1],
            all_moba[:, :, 0], all_moba[:, :, 1], *cmp_params)
        win_all = jnp.concatenate([state_nsa_win[l], jnp.stack([kw, vw], axis=2)], axis=1)
        kw_pos = past_len - state_nsa_win.shape[2] + jnp.arange(win_all.shape[1], dtype=jnp.int32)
        oa = nsa_attend(qn, qr, ga, pos_s, k_c, v_c, c_end, ks_blk, vs_blk, win_all[:, :, 0], win_all[:, :, 1], kw_pos)
        ob = moba_attend(qb, pos_s, kb_t, vb_t, kmean)
        xs = merge_branches(xs, oa, ob, gm, w_up_nsa[l], w_up_moba[l], w_out[l])
        xs = sq_relu_mlp(xs, norm_mlp[l], w_mlp_up[l], w_mlp_down[l])
        s_cmp.append(new_cmp)
        s_sel.append(new_sel)
        s_moba.append(new_moba)
        s_win.append(win_all[:, -min(WINDOW, past_len + n_new):])
    return (xp, xs, jnp.stack(p_cmp), jnp.stack(p_sel), jnp.stack(p_moba), jnp.stack(p_win),
            jnp.stack(s_cmp), jnp.stack(s_sel), jnp.stack(s_moba), jnp.stack(s_win))
```

```python
import functools

import jax
import jax.numpy as jnp
from jax import lax
from jax.experimental import pallas as pl
from jax.experimental.pallas import tpu as pltpu

F32 = jnp.float32
BF16 = jnp.bfloat16

D_MODEL = 1024
HEAD_DIM = 64
HALF = HEAD_DIM // 2
NSA_HEADS = 8
NSA_KV_GROUPS = 2
NSA_REP = NSA_HEADS // NSA_KV_GROUPS
MOBA_HEADS = 8
NSA_WIDTH = NSA_HEADS * HEAD_DIM
NSA_KV_WIDTH = NSA_KV_GROUPS * HEAD_DIM
MOBA_WIDTH = MOBA_HEADS * HEAD_DIM
CMP_LEN = 32
CMP_STRIDE = 16
CMP_HIDDEN = 2 * HEAD_DIM
SEL_BLOCK = 64
SEL_TOPK = 16
WINDOW = 512
MOBA_BLOCK = 256
MOBA_TOPK = 3
D_FF = 4 * D_MODEL
PAGE_SIZE = 128
ROPE_THETA = 10000.0
NORM_EPS = 1e-6
NEG = -1e30
FORCED = 1e6
SCALE = HEAD_DIM ** -0.5

V7X_VMEM_BYTES = 64 * 1024 * 1024
VMEM_LIMIT = V7X_VMEM_BYTES - 8 * 1024 * 1024
KV_TILE = 512
NSA_Q_TILE = 128
MOBA_Q_TILE = MOBA_BLOCK


def _cparams(n_axes):
    return pltpu.CompilerParams(dimension_semantics=("arbitrary",) * n_axes,
                                vmem_limit_bytes=VMEM_LIMIT)


def _rms_rows(x, g):
    ms = jnp.mean(x * x, axis=-1, keepdims=True)
    return x * lax.rsqrt(ms + NORM_EPS) * g


def _sigmoid(x):
    return 1.0 / (1.0 + jnp.exp(-x))


def _proj_t_kernel(*refs, norm, rope, act, outs):
    x_ref, gn_ref, w_ref = refs[:3]
    k = 3
    hg_ref = cos_ref = sin_ref = None
    if norm:
        hg_ref = refs[k]
        k += 1
    if rope:
        cos_ref, sin_ref = refs[k], refs[k + 1]
        k += 2
    out_refs = refs[k:k + len(outs)]
    ht_ref = refs[-1]

    @pl.when(pl.program_id(2) == 0)
    def _():
        h = _rms_rows(x_ref[0], gn_ref[...])
        ht_ref[...] = h.T.astype(BF16)

    acc = jnp.dot(w_ref[...], ht_ref[...], preferred_element_type=F32)
    tn = acc.shape[0]
    normed = final = acc
    if norm:
        nparts, fparts = [], []
        for hh in range(tn // HEAD_DIM):
            blk = acc[hh * HEAD_DIM:(hh + 1) * HEAD_DIM, :]
            ms = jnp.sum(blk * blk, axis=0, keepdims=True) * (1.0 / HEAD_DIM)
            y = blk * lax.rsqrt(ms + NORM_EPS) * hg_ref[hh * HEAD_DIM:(hh + 1) * HEAD_DIM, :]
            nparts.append(y)
            if rope:
                y1, y2 = y[:HALF, :], y[HALF:, :]
                c, s = cos_ref[...], sin_ref[...]
                fparts.append(y1 * c - y2 * s)
                fparts.append(y2 * c + y1 * s)
        normed = jnp.concatenate(nparts, axis=0)
        final = jnp.concatenate(fparts, axis=0) if rope else normed
    if act == "sigmoid":
        final = _sigmoid(final)
    for (stage, layout, dtype), o_ref in zip(outs, out_refs):
        val = normed if stage == "normed" else final
        if layout == "T":
            o_ref[0] = val.astype(dtype)
        elif layout == "TT":
            o_ref[0, 0] = val.astype(dtype)
        else:
            o_ref[0] = val.T.astype(dtype)


def _proj_t(x, gn, wt, *, tm, tn, outs, head_gain=None, cos=None, sin=None, act="none"):
    b, t, d = x.shape
    n = wt.shape[0]
    norm = head_gain is not None
    rope = cos is not None
    n_pos_tiles = (cos.shape[1] // tm) if rope else 1
    in_specs = [
        pl.BlockSpec((1, tm, d), lambda bb, i, j: (bb, i, 0)),
        pl.BlockSpec((1, d), lambda bb, i, j: (0, 0)),
        pl.BlockSpec((tn, d), lambda bb, i, j: (j, 0)),
    ]
    args = [x, gn.reshape(1, d), wt]
    if norm:
        in_specs.append(pl.BlockSpec((tn, 1), lambda bb, i, j: (j, 0)))
        args.append(head_gain.reshape(n, 1))
    if rope:
        in_specs += [pl.BlockSpec((HALF, tm), lambda bb, i, j: (0, i % n_pos_tiles))] * 2
        args += [cos, sin]
    out_shapes, out_specs = [], []
    for stage, layout, dtype in outs:
        if layout == "T":
            out_shapes.append(jax.ShapeDtypeStruct((b, n, t), dtype))
            out_specs.append(pl.BlockSpec((1, tn, tm), lambda bb, i, j: (bb, j, i)))
        elif layout == "TT":
            out_shapes.append(jax.ShapeDtypeStruct((b, t // tm, n, tm), dtype))
            out_specs.append(pl.BlockSpec((1, 1, tn, tm), lambda bb, i, j: (bb, i, j, 0)))
        else:
            out_shapes.append(jax.ShapeDtypeStruct((b, t, n), dtype))
            out_specs.append(pl.BlockSpec((1, tm, tn), lambda bb, i, j: (bb, i, j)))
    res = pl.pallas_call(
        functools.partial(_proj_t_kernel, norm=norm, rope=rope, act=act, outs=tuple(outs)),
        grid=(b, t // tm, n // tn),
        in_specs=in_specs,
        out_specs=out_specs,
        out_shape=out_shapes,
        scratch_shapes=[pltpu.VMEM((d, tm), BF16)],
        compiler_params=_cparams(3),
        name="proj_t",
    )(*args)
    return res


def _proj_rows_kernel(x_ref, gn_ref, w_ref, o_ref, h_ref):
    @pl.when(pl.program_id(1) == 0)
    def _():
        h_ref[...] = _rms_rows(x_ref[...], gn_ref[...]).astype(BF16)

    acc = lax.dot_general(h_ref[...], w_ref[...], (((1,), (1,)), ((), ())),
                          preferred_element_type=F32)
    o_ref[...] = _sigmoid(acc)


def _proj_rows_sigmoid(x2, gn, wt, *, tm, tn):
    m, d = x2.shape
    n = wt.shape[0]
    return pl.pallas_call(
        _proj_rows_kernel,
        grid=(m // tm, n // tn),
        in_specs=[pl.BlockSpec((tm, d), lambda i, j: (i, 0)),
                  pl.BlockSpec((1, d), lambda i, j: (0, 0)),
                  pl.BlockSpec((tn, d), lambda i, j: (j, 0))],
        out_specs=pl.BlockSpec((tm, tn), lambda i, j: (i, j)),
        out_shape=jax.ShapeDtypeStruct((m, n), F32),
        scratch_shapes=[pltpu.VMEM((tm, d), BF16)],
        compiler_params=_cparams(2),
        name="proj_gate_rows",
    )(x2, gn.reshape(1, d), wt)


def _gelu_tanh(x):
    return 0.5 * x * (1.0 + jnp.tanh(0.7978845608028654 * (x + 0.044715 * x * x * x)))


def _compress_kernel(sub_ref, w1_ref, pos_ref, w2_ref, gk_ref, kc_ref, vct_ref, *, s_pad):
    sub = sub_ref[0]
    s = sub.shape[0]
    hcat = jnp.dot(sub, w1_ref[...], preferred_element_type=F32)
    bias = jnp.dot(pos_ref[...], w1_ref[...], preferred_element_type=F32)
    hid = CMP_HIDDEN
    outs = []
    for kv in range(2):
        acc = None
        for g in range(NSA_KV_GROUPS):
            c0 = ((kv * NSA_KV_GROUPS + g) * 2) * hid
            lo = hcat[:, c0:c0 + hid]
            hi = hcat[:, c0 + hid:c0 + 2 * hid]
            b = bias[0:1, c0:c0 + hid] + bias[1:2, c0 + hid:c0 + 2 * hid]
            hidden = lo + pltpu.roll(hi, s - 1, axis=0) + b
            part = jnp.dot(_gelu_tanh(hidden).astype(BF16), w2_ref[kv * NSA_KV_GROUPS + g],
                           preferred_element_type=F32)
            acc = part if acc is None else acc + part
        if s_pad > s:
            acc = jnp.concatenate([acc, jnp.zeros((s_pad - s, acc.shape[1]), F32)], axis=0)
        outs.append(acc)
    ckt = outs[0].T
    parts = []
    for g in range(NSA_KV_GROUPS):
        blk = ckt[g * HEAD_DIM:(g + 1) * HEAD_DIM, :]
        ms = jnp.sum(blk * blk, axis=0, keepdims=True) * (1.0 / HEAD_DIM)
        parts.append(blk * lax.rsqrt(ms + NORM_EPS) * gk_ref[...])
    kc_ref[0] = jnp.concatenate(parts, axis=0).T.astype(BF16)
    vct_ref[0] = outs[1].T.astype(BF16)


def _compress(sub, w1cat, posrow, w2pad, gk, *, s_pad):
    nb, s, kdim = sub.shape
    return pl.pallas_call(
        functools.partial(_compress_kernel, s_pad=s_pad),
        grid=(nb,),
        in_specs=[pl.BlockSpec((1, s, kdim), lambda i: (i, 0, 0)),
                  pl.BlockSpec(w1cat.shape, lambda i: (0, 0)),
                  pl.BlockSpec(posrow.shape, lambda i: (0, 0)),
                  pl.BlockSpec(w2pad.shape, lambda i: (0, 0, 0)),
                  pl.BlockSpec((HEAD_DIM, 1), lambda i: (0, 0))],
        out_specs=[pl.BlockSpec((1, s_pad, NSA_KV_WIDTH), lambda i: (i, 0, 0)),
                   pl.BlockSpec((1, NSA_KV_WIDTH, s_pad), lambda i: (i, 0, 0))],
        out_shape=[jax.ShapeDtypeStruct((nb, s_pad, NSA_KV_WIDTH), BF16),
                   jax.ShapeDtypeStruct((nb, NSA_KV_WIDTH, s_pad), BF16)],
        compiler_params=_cparams(1),
        name="compress",
    )(sub, w1cat, posrow, w2pad, gk.reshape(HEAD_DIM, 1))


def _online_step(s, mask, m, l):
    sm = jnp.where(mask, s, NEG)
    m_new = jnp.maximum(m, jnp.max(sm, axis=0, keepdims=True))
    alpha = jnp.exp(m - m_new)
    p = jnp.where(mask, jnp.exp(sm - m_new), 0.0)
    l_new = alpha * l + jnp.sum(p, axis=0, keepdims=True)
    return p, alpha, m_new, l_new


def _rank_rows(score, blk):
    rank = jnp.zeros(score.shape, F32)
    for j in range(score.shape[0]):
        sj = score[j:j + 1, :]
        beats = (sj > score) | ((sj == score) & (blk > j))
        rank = rank + beats.astype(F32)
    return rank


def _rep_rows(row8, n_rows):
    return jnp.concatenate([row8] * (n_rows // 8), axis=0)


def _nsa_kernel(qn_ref, qr_ref, ga_ref, kc_ref, vct_ref, ks_ref, vst_ref, kw_ref, vwt_ref,
                o_ref, sel_ref, *, tq, tk, pos0, win_tile0, n_sel):
    qi = pl.program_id(1)
    q0 = pos0 + qi * tq
    n = NSA_REP * tq
    n_cmp = kc_ref.shape[1]
    blocks_per_tile = tk // SEL_BLOCK
    lane = lax.broadcasted_iota(jnp.int32, (1, n), 1)
    t_lane = q0 + (lane & (tq - 1))
    t_q = q0 + lax.broadcasted_iota(jnp.int32, (1, tq), 1)
    cur = t_q >> 6
    blk = lax.broadcasted_iota(jnp.int32, (n_sel, 1), 0)
    cend = lax.broadcasted_iota(jnp.int32, (n_cmp, 1), 0) * CMP_STRIDE + (CMP_LEN - 1)
    krow = lax.broadcasted_iota(jnp.int32, (tk, 1), 0)
    ci = lax.broadcasted_iota(jnp.int32, (n_sel, n_cmp), 1)
    mi = lax.broadcasted_iota(jnp.int32, (n_sel, n_cmp), 0) * (SEL_BLOCK // CMP_STRIDE)
    amat = ((ci >= mi) & (ci <= mi + 3)).astype(F32) + ((ci >= mi - 1) & (ci <= mi + 2)).astype(F32)
    zeros_q = jnp.zeros((HEAD_DIM, n), BF16)

    def q_pad(ref, g):
        q = jnp.concatenate([ref[0, (g * NSA_REP + r) * HEAD_DIM:(g * NSA_REP + r + 1) * HEAD_DIM, :]
                             for r in range(NSA_REP)], axis=1)
        return jnp.concatenate([q, zeros_q] if g == 0 else [zeros_q, q], axis=0)

    def gate_row(branch, g):
        return jnp.concatenate([ga_ref[0, branch * NSA_HEADS + g * NSA_REP + r:
                                       branch * NSA_HEADS + g * NSA_REP + r + 1, :]
                                for r in range(NSA_REP)], axis=1)

    for g in range(NSA_KV_GROUPS):
        rows = slice(g * HEAD_DIM, (g + 1) * HEAD_DIM)
        qn = q_pad(qn_ref, g)
        qr = q_pad(qr_ref, g)
        sc = jnp.dot(kc_ref[0], qn, preferred_element_type=F32) * SCALE
        mask_c = cend <= t_lane
        smc = jnp.where(mask_c, sc, NEG)
        e = jnp.where(mask_c, jnp.exp(smc - jnp.max(smc, axis=0, keepdims=True)), 0.0)
        den = jnp.sum(e, axis=0, keepdims=True)
        pc = e / jnp.where(den > 0.0, den, 1.0)
        o_c = jnp.dot(vct_ref[0], pc.astype(BF16), preferred_element_type=F32)[rows, :]
        imp_n = jnp.dot(amat, pc, preferred_element_type=F32, precision=lax.Precision.HIGHEST)
        imp = imp_n[:, 0:tq]
        for r in range(1, NSA_REP):
            imp = imp + imp_n[:, r * tq:(r + 1) * tq]
        forced = (blk == 0) | (blk >= cur - 1)
        score = jnp.where(blk > cur, NEG, jnp.where(forced, FORCED, imp))
        sel = ((_rank_rows(score, blk) < SEL_TOPK) & (blk <= cur)).astype(F32)
        sel_n = jnp.concatenate([sel] * NSA_REP, axis=1)
        for m_ in range(n_sel):
            sel_ref[m_] = jnp.broadcast_to(sel_n[m_:m_ + 1, :], (8, n))

        def sel_body(kv, carry):
            m, l, acc = carry
            kt = ks_ref[0, pl.ds(pl.multiple_of(kv * tk, tk), tk), :]
            s = jnp.dot(kt, qr, preferred_element_type=F32) * SCALE
            bmask = jnp.concatenate(
                [_rep_rows(sel_ref[kv * blocks_per_tile + b_], SEL_BLOCK) for b_ in range(blocks_per_tile)],
                axis=0) > 0.5
            mask = bmask & ((kv * tk + krow) <= t_lane)
            p, alpha, m, l = _online_step(s, mask, m, l)
            pv = jnp.dot(vst_ref[0, kv], p.astype(BF16), preferred_element_type=F32)[rows, :]
            return m, l, alpha * acc + pv

        init = (jnp.full((1, n), NEG, F32), jnp.zeros((1, n), F32), jnp.zeros((HEAD_DIM, n), F32))
        n_kv = (q0 + tq + tk - 1) // tk
        _, l_s, acc_s = lax.fori_loop(0, n_kv, sel_body, init)
        o_s = acc_s / l_s

        def win_body(kv, carry):
            m, l, acc = carry
            kt = kw_ref[0, pl.ds(pl.multiple_of((kv - win_tile0) * tk, tk), tk), :]
            s = jnp.dot(kt, qr, preferred_element_type=F32) * SCALE
            rel = t_lane - (kv * tk + krow)
            mask = (rel >= 0) & (rel < WINDOW)
            p, alpha, m, l = _online_step(s, mask, m, l)
            pv = jnp.dot(vwt_ref[0, kv - win_tile0], p.astype(BF16), preferred_element_type=F32)[rows, :]
            return m, l, alpha * acc + pv

        w1 = q0 // tk
        _, l_w, acc_w = lax.fori_loop(jnp.maximum(w1 - 1, win_tile0), w1 + 1, win_body, init)
        o_w = acc_w / l_w

        o = gate_row(0, g) * o_c + gate_row(1, g) * o_s + gate_row(2, g) * o_w
        o_heads = jnp.concatenate([o[:, r * tq:(r + 1) * tq] for r in range(NSA_REP)], axis=0)
        o_ref[0, :, g * NSA_REP * HEAD_DIM:(g + 1) * NSA_REP * HEAD_DIM] = o_heads.T.astype(BF16)


def _nsa(qn_t, qr_t, ga_t, kc, vc_t, k_rows, ks_col, vs_tt, vs_row, kw_rows, kw_col, vw_tt, vw_row,
         *, pos0, win_tile0):
    b, _, tq_total = qn_t.shape
    tq, tk = NSA_Q_TILE, KV_TILE
    n_cmp = kc.shape[1]
    l_sel = k_rows.shape[1]
    n_sel = ((l_sel // SEL_BLOCK) + 7) // 8 * 8
    l_win = kw_rows.shape[1]
    n = NSA_REP * tq
    return pl.pallas_call(
        functools.partial(_nsa_kernel, tq=tq, tk=tk, pos0=pos0, win_tile0=win_tile0, n_sel=n_sel),
        grid=(b, tq_total // tq),
        in_specs=[
            pl.BlockSpec((1, NSA_WIDTH, tq), lambda bb, i: (bb, 0, i)),
            pl.BlockSpec((1, NSA_WIDTH, tq), lambda bb, i: (bb, 0, i)),
            pl.BlockSpec((1, ga_t.shape[1], tq), lambda bb, i: (bb, 0, i)),
            pl.BlockSpec((1, n_cmp, NSA_KV_WIDTH), lambda bb, i: (bb, 0, 0)),
            pl.BlockSpec((1, NSA_KV_WIDTH, n_cmp), lambda bb, i: (bb, 0, 0)),
            pl.BlockSpec((1, l_sel, NSA_KV_WIDTH), lambda bb, i: (bb, 0, ks_col)),
            pl.BlockSpec((1, l_sel // tk, NSA_KV_WIDTH, tk), lambda bb, i: (bb, 0, vs_row, 0)),
            pl.BlockSpec((1, l_win, NSA_KV_WIDTH), lambda bb, i: (bb, 0, kw_col)),
            pl.BlockSpec((1, l_win // tk, NSA_KV_WIDTH, tk), lambda bb, i: (bb, 0, vw_row, 0)),
        ],
        out_specs=pl.BlockSpec((1, tq, NSA_WIDTH), lambda bb, i: (bb, i, 0)),
        out_shape=jax.ShapeDtypeStruct((b, tq_total, NSA_WIDTH), BF16),
        scratch_shapes=[pltpu.VMEM((n_sel, 8, n), F32)],
        compiler_params=_cparams(2),
        name="nsa_attend",
    )(qn_t, qr_t, ga_t, kc, vc_t, k_rows, vs_tt, kw_rows, vw_tt)


def _moba_kernel(q_ref, k_ref, vt_ref, o_ref, km_ref, sel_ref, *, tq, tk, pos0, n_blk):
    qi = pl.program_id(2)
    q0 = pos0 + qi * tq
    n = 2 * tq
    cq = q0 // MOBA_BLOCK
    blocks_per_tile = tk // MOBA_BLOCK
    n_rows = km_ref.shape[0]

    @pl.when(qi == 0)
    def _():
        kall = k_ref[0].astype(F32)
        km = jnp.sum(kall.reshape(n_blk, MOBA_BLOCK, 2 * HEAD_DIM), axis=1) * (1.0 / MOBA_BLOCK)
        if n_rows > n_blk:
            km = jnp.concatenate([km, jnp.zeros((n_rows - n_blk, 2 * HEAD_DIM), F32)], axis=0)
        km_ref[...] = km.astype(BF16)

    lane = lax.broadcasted_iota(jnp.int32, (1, n), 1)
    t_lane = q0 + (lane & (tq - 1))
    blk = lax.broadcasted_iota(jnp.int32, (n_rows, 1), 0)
    krow = lax.broadcasted_iota(jnp.int32, (MOBA_BLOCK, 1), 0)
    zq = jnp.zeros((HEAD_DIM, tq), BF16)
    qbd = jnp.concatenate([jnp.concatenate([q_ref[0, 0:HEAD_DIM, :], zq], axis=0),
                           jnp.concatenate([zq, q_ref[0, HEAD_DIM:2 * HEAD_DIM, :]], axis=0)], axis=1)
    s_blk = jnp.dot(km_ref[...], qbd, preferred_element_type=F32)
    past = blk < cq
    score = jnp.where(past, s_blk, NEG)
    sel = ((_rank_rows(score, blk) < MOBA_TOPK) & past).astype(F32)
    for m_ in range(n_rows):
        sel_ref[m_] = jnp.broadcast_to(sel[m_:m_ + 1, :], (8, n))

    def body(kv, carry):
        m, l, acc = carry
        kt = k_ref[0, pl.ds(pl.multiple_of(kv * tk, tk), tk), :]
        s = jnp.dot(kt, qbd, preferred_element_type=F32) * SCALE
        parts = []
        for b_ in range(blocks_per_tile):
            nb = kv * blocks_per_tile + b_
            own = (nb == cq) & ((nb * MOBA_BLOCK + krow) <= t_lane)
            parts.append((_rep_rows(sel_ref[nb], MOBA_BLOCK) > 0.5) | own)
        mask = jnp.concatenate(parts, axis=0)
        p, alpha, m, l = _online_step(s, mask, m, l)
        pv = jnp.dot(vt_ref[0, kv], p.astype(BF16), preferred_element_type=F32)
        return m, l, alpha * acc + pv

    init = (jnp.full((1, n), NEG, F32), jnp.zeros((1, n), F32), jnp.zeros((2 * HEAD_DIM, n), F32))
    n_kv = (q0 + tq + tk - 1) // tk
    _, l, acc = lax.fori_loop(0, n_kv, body, init)
    o = acc / l
    o2 = jnp.concatenate([o[0:HEAD_DIM, 0:tq], o[HEAD_DIM:2 * HEAD_DIM, tq:2 * tq]], axis=0)
    o_ref[0] = o2.T.astype(BF16)


def _moba(q_t, k_rows, k_col0, v_tt, v_row0, *, pos0):
    b, _, tq_total = q_t.shape
    tq, tk = MOBA_Q_TILE, KV_TILE
    l_kv = k_rows.shape[1]
    n_blk = l_kv // MOBA_BLOCK
    n_rows = (n_blk + 7) // 8 * 8
    n = 2 * tq
    pair = 2 * HEAD_DIM
    return pl.pallas_call(
        functools.partial(_moba_kernel, tq=tq, tk=tk, pos0=pos0, n_blk=n_blk),
        grid=(b, MOBA_HEADS // 2, tq_total // tq),
        in_specs=[
            pl.BlockSpec((1, pair, tq), lambda bb, hp, i: (bb, hp, i)),
            pl.BlockSpec((1, l_kv, pair), lambda bb, hp, i: (bb, 0, k_col0 + hp)),
            pl.BlockSpec((1, l_kv // tk, pair, tk), lambda bb, hp, i: (bb, 0, v_row0 + hp, 0)),
        ],
        out_specs=pl.BlockSpec((1, tq, pair), lambda bb, hp, i: (bb, i, hp)),
        out_shape=jax.ShapeDtypeStruct((b, tq_total, MOBA_WIDTH), BF16),
        scratch_shapes=[pltpu.VMEM((n_rows, pair), BF16), pltpu.VMEM((n_rows, 8, n), F32)],
        compiler_params=_cparams(3),
        name="moba_attend",
    )(q_t, k_rows, v_tt)


def _merge_kernel(x_ref, oa_ref, ob_ref, gm_ref, wa_ref, wb_ref, wo_ref, o_ref):
    ua = jnp.dot(oa_ref[...], wa_ref[...], preferred_element_type=F32)
    ub = jnp.dot(ob_ref[...], wb_ref[...], preferred_element_type=F32)
    mixed = gm_ref[:, :D_MODEL] * ua + gm_ref[:, D_MODEL:] * ub
    o_ref[...] = x_ref[...] + jnp.dot(mixed.astype(BF16), wo_ref[...], preferred_element_type=F32)


def _merge(x2, oa, ob, gm, wa, wb, wo, *, tm):
    m, d = x2.shape
    row = lambda i: (i, 0)
    fixed = lambda i: (0, 0)
    return pl.pallas_call(
        _merge_kernel,
        grid=(m // tm,),
        in_specs=[pl.BlockSpec((tm, d), row), pl.BlockSpec((tm, NSA_WIDTH), row),
                  pl.BlockSpec((tm, MOBA_WIDTH), row), pl.BlockSpec((tm, 2 * d), row),
                  pl.BlockSpec(wa.shape, fixed), pl.BlockSpec(wb.shape, fixed), pl.BlockSpec(wo.shape, fixed)],
        out_specs=pl.BlockSpec((tm, d), row),
        out_shape=jax.ShapeDtypeStruct((m, d), F32),
        compiler_params=_cparams(1),
        name="merge_out",
    )(x2, oa, ob, gm, wa, wb, wo)


def _mlp_kernel(x_ref, g_ref, w1_ref, w2_ref, o_ref, h_ref, acc_ref):
    c = pl.program_id(1)

    @pl.when(c == 0)
    def _():
        h_ref[...] = _rms_rows(x_ref[...], g_ref[...]).astype(BF16)
        acc_ref[...] = x_ref[...]

    u = jnp.maximum(jnp.dot(h_ref[...], w1_ref[...], preferred_element_type=F32), 0.0)
    acc_ref[...] += jnp.dot((u * u).astype(BF16), w2_ref[...], preferred_element_type=F32)

    @pl.when(c == pl.num_programs(1) - 1)
    def _():
        o_ref[...] = acc_ref[...]


def _mlp(x2, g, w1, w2, *, tm, tf):
    m, d = x2.shape
    dff = w1.shape[1]
    return pl.pallas_call(
        _mlp_kernel,
        grid=(m // tm, dff // tf),
        in_specs=[pl.BlockSpec((tm, d), lambda i, c: (i, 0)),
                  pl.BlockSpec((1, d), lambda i, c: (0, 0)),
                  pl.BlockSpec((d, tf), lambda i, c: (0, c)),
                  pl.BlockSpec((tf, d), lambda i, c: (c, 0))],
        out_specs=pl.BlockSpec((tm, d), lambda i, c: (i, 0)),
        out_shape=jax.ShapeDtypeStruct((m, d), F32),
        scratch_shapes=[pltpu.VMEM((tm, d), BF16), pltpu.VMEM((tm, d), F32)],
        compiler_params=_cparams(2),
        name="sq_relu_mlp",
    )(x2, g.reshape(1, d), w1, w2)


def _repack_kernel(pt_ref, *refs, n_pages, page_rows, feats, l_out, tk, all_rows):
    del pt_ref
    page_refs = refs[:n_pages]
    new_ref = refs[n_pages]
    outs = refs[n_pages + 1:]
    past = n_pages * page_rows
    tail = l_out - past
    new_row = new_ref[0]
    tail_rows = lax.broadcasted_iota(jnp.int32, (tail, 1), 0)
    if all_rows:
        (rows_ref,) = outs
        for p in range(n_pages):
            page = page_refs[p][...].reshape(2 * feats, page_rows)
            rows_ref[0, p * page_rows:(p + 1) * page_rows, :] = page.T.astype(BF16)
        rows_ref[0, past:l_out, :] = jnp.where(tail_rows == 0, new_row, 0.0).astype(BF16)
        return
    k_ref, vt_ref = outs
    per_tile = tk // page_rows
    for p in range(n_pages):
        page = page_refs[p][...]
        k_ref[0, p * page_rows:(p + 1) * page_rows, :] = page[0].reshape(feats, page_rows).T.astype(BF16)
        vt_ref[0, p // per_tile, :, (p % per_tile) * page_rows:(p % per_tile + 1) * page_rows] = (
            page[1].reshape(feats, page_rows).astype(BF16))
    k_ref[0, past:l_out, :] = jnp.where(tail_rows == 0, new_row[:, :feats], 0.0).astype(BF16)
    v_new_t = jnp.broadcast_to(new_row[:, feats:], (128, feats)).T
    lane = lax.broadcasted_iota(jnp.int32, (1, 128), 1)
    first = jnp.where(lane == 0, v_new_t, 0.0).astype(BF16)
    for tt in range(tail // tk):
        for c in range(tk // 128):
            blk = first if (tt == 0 and c == 0) else jnp.zeros((feats, 128), BF16)
            vt_ref[0, past // tk + tt, :, c * 128:(c + 1) * 128] = blk


def _repack(cache_t, layer, page_table, new_rows, *, l_out, all_rows):
    _, _, _, h, _, page_rows = cache_t.shape
    feats = h * HEAD_DIM
    n_seq = new_rows.shape[0]
    tk = KV_TILE
    if page_table is None:
        n_pages = 1
        page_table = jnp.arange(n_seq, dtype=jnp.int32).reshape(n_seq, 1)
    else:
        n_pages = page_table.shape[1]

    def page_spec(p):
        return pl.BlockSpec((None, None, 2, h, HEAD_DIM, page_rows),
                            lambda s, pt: (layer, pt[s, p], 0, 0, 0, 0))

    in_specs = [page_spec(p) for p in range(n_pages)]
    in_specs.append(pl.BlockSpec((1, 1, 2 * feats), lambda s, pt: (s, 0, 0)))
    if all_rows:
        out_shape = [jax.ShapeDtypeStruct((n_seq, l_out, 2 * feats), BF16)]
        out_specs = [pl.BlockSpec((1, l_out, 2 * feats), lambda s, pt: (s, 0, 0))]
    else:
        out_shape = [jax.ShapeDtypeStruct((n_seq, l_out, feats), BF16),
                     jax.ShapeDtypeStruct((n_seq, l_out // tk, feats, tk), BF16)]
        out_specs = [pl.BlockSpec((1, l_out, feats), lambda s, pt: (s, 0, 0)),
                     pl.BlockSpec((1, l_out // tk, feats, tk), lambda s, pt: (s, 0, 0, 0))]
    return pl.pallas_call(
        functools.partial(_repack_kernel, n_pages=n_pages, page_rows=page_rows, feats=feats,
                          l_out=l_out, tk=tk, all_rows=all_rows),
        grid_spec=pltpu.PrefetchScalarGridSpec(
            num_scalar_prefetch=1, grid=(n_seq,), in_specs=in_specs, out_specs=out_specs),
        out_shape=out_shape,
        compiler_params=_cparams(1),
        name="repack_pages",
    )(page_table, *([cache_t] * n_pages), new_rows)


_SPLITS = (NSA_WIDTH, NSA_KV_WIDTH, NSA_KV_WIDTH, NSA_KV_WIDTH, NSA_KV_WIDTH, NSA_KV_WIDTH, NSA_KV_WIDTH,
           3 * NSA_HEADS, MOBA_WIDTH, MOBA_WIDTH, MOBA_WIDTH, 2 * D_MODEL)
_NAMES = ("qa", "kc", "vc", "ks", "vs", "kw", "vw", "ga", "qb", "kb", "vb", "gm")


def _layer_params(l, w_in, gq_nsa, gk_sel, gk_win, gq_moba, gk_moba, cmp_pos_k, cmp_pos_v,
                  cmp_w1_k, cmp_w2_k, cmp_w1_v, cmp_w2_v):
    wt = w_in[l].T
    off, part = 0, {}
    for name, size in zip(_NAMES, _SPLITS):
        part[name] = wt[off:off + size]
        off += size
    cat = lambda *names: jnp.concatenate([part[n_] for n_ in names], axis=0).astype(BF16)
    tile = lambda g_, reps: jnp.tile(g_, reps)
    ga_pad = jnp.concatenate([part["ga"], jnp.zeros((HEAD_DIM - 3 * NSA_HEADS, D_MODEL), F32)], axis=0)
    p = {
        "w_q": cat("qa"), "g_q": tile(gq_nsa[l], NSA_HEADS),
        "w_kr": cat("ks", "kw", "kb"),
        "g_kr": jnp.concatenate([tile(gk_sel[l], NSA_KV_GROUPS), tile(gk_win[l], NSA_KV_GROUPS),
                                 tile(gk_moba[l], MOBA_HEADS)]),
        "w_qb": cat("qb"), "g_qb": tile(gq_moba[l], MOBA_HEADS),
        "w_c": cat("kc", "vc"),
        "w_v": cat("vs", "vw", "vb"),
        "w_ga": ga_pad.astype(BF16),
        "w_gm": cat("gm"),
    }
    w1cat = jnp.zeros((CMP_STRIDE, 2, NSA_KV_GROUPS, HEAD_DIM, 2, NSA_KV_GROUPS, 2, CMP_HIDDEN), F32)
    posrow = jnp.zeros((8, CMP_STRIDE, 2, NSA_KV_GROUPS, HEAD_DIM), F32)
    w2pad = jnp.zeros((2, NSA_KV_GROUPS, CMP_HIDDEN, NSA_KV_GROUPS, HEAD_DIM), F32)
    for kv, (w1, w2, pos) in enumerate(((cmp_w1_k[l], cmp_w2_k[l], cmp_pos_k[l]),
                                        (cmp_w1_v[l], cmp_w2_v[l], cmp_pos_v[l]))):
        w1r = w1.reshape(2, CMP_STRIDE, HEAD_DIM, CMP_HIDDEN)
        posr = pos.reshape(2, CMP_STRIDE, HEAD_DIM)
        for g in range(NSA_KV_GROUPS):
            for half in range(2):
                w1cat = w1cat.at[:, kv, g, :, kv, g, half, :].set(w1r[half])
                posrow = posrow.at[half, :, kv, g, :].set(posr[half])
            w2pad = w2pad.at[kv, g, :, g, :].set(w2)
    p["w1cat"] = w1cat.reshape(CMP_STRIDE * 2 * NSA_KV_WIDTH, 2 * NSA_KV_GROUPS * 2 * CMP_HIDDEN).astype(BF16)
    p["posrow"] = posrow.reshape(8, CMP_STRIDE * 2 * NSA_KV_WIDTH).astype(BF16)
    p["w2pad"] = w2pad.reshape(2 * NSA_KV_GROUPS, CMP_HIDDEN, NSA_KV_WIDTH).astype(BF16)
    return p


def _rope_tables(pos):
    inv = ROPE_THETA ** (-jnp.arange(HALF, dtype=F32) / HALF)
    ang = inv[:, None] * pos.astype(F32)[None, :]
    return jnp.cos(ang), jnp.sin(ang)


def _project_all(x3, p, gn, cos, sin, tm, *, attention_copies):
    t_f32, t_bf, r_bf = ("final", "T", F32), ("final", "T", BF16), ("final", "R", BF16)
    extra = lambda o: (t_f32, o) if attention_copies else (t_f32,)
    out = {}
    out["qn_t"], out["qr_t"] = _proj_t(x3, gn, p["w_q"], tm=tm, tn=512, head_gain=p["g_q"], cos=cos, sin=sin,
                                       outs=(("normed", "T", BF16), t_bf))
    out["kr_t"], *rest = _proj_t(x3, gn, p["w_kr"], tm=tm, tn=256, head_gain=p["g_kr"], cos=cos, sin=sin,
                                 outs=extra(r_bf))
    out["kr_r"] = rest[0] if rest else None
    (out["qb_t"],) = _proj_t(x3, gn, p["w_qb"], tm=tm, tn=512, head_gain=p["g_qb"], cos=cos, sin=sin,
                             outs=(t_bf,))
    out["c_t"], *rest = _proj_t(x3, gn, p["w_c"], tm=tm, tn=256, outs=extra(r_bf))
    out["c_r"] = rest[0] if rest else None
    out["v_t"], *rest = _proj_t(x3, gn, p["w_v"], tm=tm, tn=256, outs=extra(("final", "TT", BF16)))
    out["v_x"] = rest[0] if rest else None
    (out["ga_t"],) = _proj_t(x3, gn, p["w_ga"], tm=tm, tn=HEAD_DIM, act="sigmoid", outs=(t_f32,))
    b, t, d = x3.shape
    out["gm"] = _proj_rows_sigmoid(x3.reshape(b * t, d), gn, p["w_gm"], tm=tm, tn=512)
    return out


def _cache_leaf(feat_major, heads):
    b, _, t = feat_major.shape
    return feat_major.reshape(b, 2, heads, HEAD_DIM, t).transpose(0, 4, 1, 2, 3)


def kernel(x_prompt, x_sample, cache_nsa_cmp, cache_nsa_sel, cache_moba, state_nsa_win, page_table,
           norm_mix, w_in, gq_nsa, gk_cmp, gk_sel, gk_win, gq_moba, gk_moba,
           cmp_pos_k, cmp_pos_v, cmp_w1_k, cmp_w2_k, cmp_w1_v, cmp_w2_v,
           w_up_nsa, w_up_moba, w_out, norm_mlp, w_mlp_up, w_mlp_down):
    depth = w_in.shape[0]
    b, t, d = x_prompt.shape
    n_dec = x_sample.shape[0]
    past_len = page_table.shape[1] * PAGE_SIZE
    w_buf = state_nsa_win.shape[2]
    tk = KV_TILE
    l_dec = (past_len + 1 + tk - 1) // tk * tk
    win_tile0 = (past_len - w_buf) // tk

    cos_p, sin_p = _rope_tables(jnp.arange(t, dtype=jnp.int32))
    cos_s, sin_s = _rope_tables(jnp.full((n_dec,), past_len, dtype=jnp.int32))
    to_t = lambda c: c.transpose(0, 1, 3, 4, 5, 2)
    cmp_t, sel_t, moba_t, win_t = to_t(cache_nsa_cmp), to_t(cache_nsa_sel), to_t(cache_moba), to_t(state_nsa_win)

    xp = x_prompt
    xs = x_sample.reshape(1, n_dec, d)
    leaves = [[] for _ in range(8)]
    tm_p = 512
    for l in range(depth):
        p = _layer_params(l, w_in, gq_nsa, gk_sel, gk_win, gq_moba, gk_moba, cmp_pos_k, cmp_pos_v,
                          cmp_w1_k, cmp_w2_k, cmp_w1_v, cmp_w2_v)
        wa, wb, wo = w_up_nsa[l].astype(BF16), w_up_moba[l].astype(BF16), w_out[l].astype(BF16)
        w1, w2 = w_mlp_up[l].astype(BF16), w_mlp_down[l].astype(BF16)

        pr = _project_all(xp, p, norm_mix[l], cos_p, sin_p, tm_p, attention_copies=True)
        sub = pr["c_r"].reshape(b, t // CMP_STRIDE, CMP_STRIDE * 2 * NSA_KV_WIDTH)
        kc, vc_t = _compress(sub, p["w1cat"], p["posrow"], p["w2pad"], gk_cmp[l], s_pad=t // CMP_STRIDE)
        oa = _nsa(pr["qn_t"], pr["qr_t"], pr["ga_t"], kc, vc_t,
                  pr["kr_r"], 0, pr["v_x"], 0, pr["kr_r"], 1, pr["v_x"], 1, pos0=0, win_tile0=0)
        ob = _moba(pr["qb_t"], pr["kr_r"], 2, pr["v_x"], 2, pos0=0)
        x2 = _merge(xp.reshape(b * t, d), oa.reshape(b * t, NSA_WIDTH), ob.reshape(b * t, MOBA_WIDTH),
                    pr["gm"], wa, wb, wo, tm=tm_p)
        xp = _mlp(x2, norm_mlp[l], w1, w2, tm=tm_p, tf=1024).reshape(b, t, d)
        kr_t, v_t = pr["kr_t"], pr["v_t"]
        kvw = NSA_KV_WIDTH
        leaves[0].append(_cache_leaf(pr["c_t"], NSA_KV_GROUPS))
        leaves[1].append(_cache_leaf(jnp.concatenate([kr_t[:, :kvw], v_t[:, :kvw]], axis=1), NSA_KV_GROUPS))
        leaves[2].append(_cache_leaf(jnp.concatenate([kr_t[:, 2 * kvw:], v_t[:, 2 * kvw:]], axis=1), MOBA_HEADS))
        w_keep = min(WINDOW, t)
        leaves[3].append(_cache_leaf(jnp.concatenate([kr_t[:, kvw:2 * kvw, t - w_keep:],
                                                      v_t[:, kvw:2 * kvw, t - w_keep:]], axis=1), NSA_KV_GROUPS))

        sr = _project_all(xs, p, norm_mix[l], cos_s, sin_s, n_dec, attention_copies=False)
        kr_new = sr["kr_t"][0].T
        c_new, v_new = sr["c_t"][0].T, sr["v_t"][0].T
        new_sel = jnp.concatenate([kr_new[:, :kvw], v_new[:, :kvw]], axis=1)
        new_win = jnp.concatenate([kr_new[:, kvw:2 * kvw], v_new[:, kvw:2 * kvw]], axis=1)
        new_moba = jnp.concatenate([kr_new[:, 2 * kvw:], v_new[:, 2 * kvw:]], axis=1)
        (cmp_rows,) = _repack(cmp_t, l, page_table, c_new[:, None, :], l_out=l_dec, all_rows=True)
        ks_rows, vs_tt = _repack(sel_t, l, page_table, new_sel[:, None, :], l_out=l_dec, all_rows=False)
        kb_rows, vb_tt = _repack(moba_t, l, page_table, new_moba[:, None, :], l_out=l_dec, all_rows=False)
        kw_rows, vw_tt = _repack(win_t, l, None, new_win[:, None, :], l_out=l_dec - win_tile0 * tk, all_rows=False)
        sub = cmp_rows.reshape(n_dec, l_dec // CMP_STRIDE, CMP_STRIDE * 2 * NSA_KV_WIDTH)
        s_pad = (l_dec // CMP_STRIDE + 127) // 128 * 128
        kc, vc_t = _compress(sub, p["w1cat"], p["posrow"], p["w2pad"], gk_cmp[l], s_pad=s_pad)
        lanes = lambda a, width: jnp.pad(a[0].T[:, :, None], ((0, 0), (0, 0), (0, width - 1)))
        oa = _nsa(lanes(sr["qn_t"], NSA_Q_TILE), lanes(sr["qr_t"], NSA_Q_TILE), lanes(sr["ga_t"], NSA_Q_TILE),
                  kc, vc_t, ks_rows, 0, vs_tt, 0, kw_rows, 0, vw_tt, 0, pos0=past_len, win_tile0=win_tile0)[:, 0]
        ob = _moba(lanes(sr["qb_t"], MOBA_Q_TILE), kb_rows, 0, vb_tt, 0, pos0=past_len)[:, 0]
        x2 = _merge(xs.reshape(n_dec, d), oa, ob, sr["gm"], wa, wb, wo, tm=n_dec)
        xs = _mlp(x2, norm_mlp[l], w1, w2, tm=n_dec, tf=1024).reshape(1, n_dec, d)
        seq_leaf = lambda rows, heads: rows.reshape(n_dec, 1, 2, heads, HEAD_DIM)
        leaves[4].append(seq_leaf(c_new, NSA_KV_GROUPS))
        leaves[5].append(seq_leaf(new_sel, NSA_KV_GROUPS))
        leaves[6].append(seq_leaf(new_moba, MOBA_HEADS))
        win_all = jnp.concatenate([state_nsa_win[l], seq_leaf(new_win, NSA_KV_GROUPS)], axis=1)
        leaves[7].append(win_all[:, -min(WINDOW, past_len + 1):])
    return (xp, xs.reshape(n_dec, 1, d)) + tuple(jnp.stack(v) for v in leaves)
```

```python
import functools

import jax
import jax.numpy as jnp
from jax import lax
from jax.experimental import pallas as pl
from jax.experimental.pallas import tpu as pltpu

F32 = jnp.float32
BF16 = jnp.bfloat16

D_MODEL = 1024
HEAD_DIM = 64
HALF = HEAD_DIM // 2
NSA_HEADS = 8
NSA_KV_GROUPS = 2
NSA_REP = NSA_HEADS // NSA_KV_GROUPS
MOBA_HEADS = 8
NSA_WIDTH = NSA_HEADS * HEAD_DIM
NSA_KV_WIDTH = NSA_KV_GROUPS * HEAD_DIM
MOBA_WIDTH = MOBA_HEADS * HEAD_DIM
CMP_LEN = 32
CMP_STRIDE = 16
CMP_HIDDEN = 2 * HEAD_DIM
SEL_BLOCK = 64
SEL_TOPK = 16
WINDOW = 512
MOBA_BLOCK = 256
MOBA_TOPK = 3
D_FF = 4 * D_MODEL
PAGE_SIZE = 128
ROPE_THETA = 10000.0
NORM_EPS = 1e-6
NEG = -1e30
FORCED = 1e6
SCALE = HEAD_DIM ** -0.5

V7X_VMEM_BYTES = 64 * 1024 * 1024
VMEM_LIMIT = V7X_VMEM_BYTES - 8 * 1024 * 1024
KV_TILE = 512
NSA_Q_TILE = 128
MOBA_Q_TILE = MOBA_BLOCK


def _cparams(n_axes):
    return pltpu.CompilerParams(dimension_semantics=("arbitrary",) * n_axes,
                                vmem_limit_bytes=VMEM_LIMIT)


def _rms_rows(x, g):
    ms = jnp.mean(x * x, axis=-1, keepdims=True)
    return x * lax.rsqrt(ms + NORM_EPS) * g


def _sigmoid(x):
    return 1.0 / (1.0 + jnp.exp(-x))


def _proj_t_kernel(*refs, norm, rope, act, outs):
    x_ref, gn_ref, w_ref = refs[:3]
    k = 3
    hg_ref = cos_ref = sin_ref = None
    if norm:
        hg_ref = refs[k]
        k += 1
    if rope:
        cos_ref, sin_ref = refs[k], refs[k + 1]
        k += 2
    out_refs = refs[k:k + len(outs)]
    ht_ref = refs[-1]

    @pl.when(pl.program_id(2) == 0)
    def _():
        h = _rms_rows(x_ref[0], gn_ref[...])
        ht_ref[...] = h.T.astype(BF16)

    acc = jnp.dot(w_ref[...], ht_ref[...], preferred_element_type=F32)
    tn = acc.shape[0]
    normed = final = acc
    if norm:
        nparts, fparts = [], []
        for hh in range(tn // HEAD_DIM):
            blk = acc[hh * HEAD_DIM:(hh + 1) * HEAD_DIM, :]
            ms = jnp.sum(blk * blk, axis=0, keepdims=True) * (1.0 / HEAD_DIM)
            y = blk * lax.rsqrt(ms + NORM_EPS) * hg_ref[hh * HEAD_DIM:(hh + 1) * HEAD_DIM, :]
            nparts.append(y)
            if rope:
                y1, y2 = y[:HALF, :], y[HALF:, :]
                c, s = cos_ref[...], sin_ref[...]
                fparts.append(y1 * c - y2 * s)
                fparts.append(y2 * c + y1 * s)
        normed = jnp.concatenate(nparts, axis=0)
        final = jnp.concatenate(fparts, axis=0) if rope else normed
    if act == "sigmoid":
        final = _sigmoid(final)
    for (stage, layout, dtype), o_ref in zip(outs, out_refs):
        val = normed if stage == "normed" else final
        if layout == "T":
            o_ref[0] = val.astype(dtype)
        elif layout == "TT":
            o_ref[0, 0] = val.astype(dtype)
        else:
            o_ref[0] = val.T.astype(dtype)


def _proj_t(x, gn, wt, *, tm, tn, outs, head_gain=None, cos=None, sin=None, act="none"):
    b, t, d = x.shape
    n = wt.shape[0]
    norm = head_gain is not None
    rope = cos is not None
    n_pos_tiles = (cos.shape[1] // tm) if rope else 1
    in_specs = [
        pl.BlockSpec((1, tm, d), lambda bb, i, j: (bb, i, 0)),
        pl.BlockSpec((1, d), lambda bb, i, j: (0, 0)),
        pl.BlockSpec((tn, d), lambda bb, i, j: (j, 0)),
    ]
    args = [x, gn.reshape(1, d), wt]
    if norm:
        in_specs.append(pl.BlockSpec((tn, 1), lambda bb, i, j: (j, 0)))
        args.append(head_gain.reshape(n, 1))
    if rope:
        in_specs += [pl.BlockSpec((HALF, tm), lambda bb, i, j: (0, i % n_pos_tiles))] * 2
        args += [cos, sin]
    out_shapes, out_specs = [], []
    for stage, layout, dtype in outs:
        if layout == "T":
            out_shapes.append(jax.ShapeDtypeStruct((b, n, t), dtype))
            out_specs.append(pl.BlockSpec((1, tn, tm), lambda bb, i, j: (bb, j, i)))
        elif layout == "TT":
            out_shapes.append(jax.ShapeDtypeStruct((b, t // tm, n, tm), dtype))
            out_specs.append(pl.BlockSpec((1, 1, tn, tm), lambda bb, i, j: (bb, i, j, 0)))
        else:
            out_shapes.append(jax.ShapeDtypeStruct((b, t, n), dtype))
            out_specs.append(pl.BlockSpec((1, tm, tn), lambda bb, i, j: (bb, i, j)))
    res = pl.pallas_call(
        functools.partial(_proj_t_kernel, norm=norm, rope=rope, act=act, outs=tuple(outs)),
        grid=(b, t // tm, n // tn),
        in_specs=in_specs,
        out_specs=out_specs,
        out_shape=out_shapes,
        scratch_shapes=[pltpu.VMEM((d, tm), BF16)],
        compiler_params=_cparams(3),
        name="proj_t",
    )(*args)
    return res


def _proj_rows_kernel(x_ref, gn_ref, w_ref, o_ref, h_ref):
    @pl.when(pl.program_id(1) == 0)
    def _():
        h_ref[...] = _rms_rows(x_ref[...], gn_ref[...]).astype(BF16)

    acc = lax.dot_general(h_ref[...], w_ref[...], (((1,), (1,)), ((), ())),
                          preferred_element_type=F32)
    o_ref[...] = _sigmoid(acc)


def _proj_rows_sigmoid(x2, gn, wt, *, tm, tn):
    m, d = x2.shape
    n = wt.shape[0]
    return pl.pallas_call(
        _proj_rows_kernel,
        grid=(m // tm, n // tn),
        in_specs=[pl.BlockSpec((tm, d), lambda i, j: (i, 0)),
                  pl.BlockSpec((1, d), lambda i, j: (0, 0)),
                  pl.BlockSpec((tn, d), lambda i, j: (j, 0))],
        out_specs=pl.BlockSpec((tm, tn), lambda i, j: (i, j)),
        out_shape=jax.ShapeDtypeStruct((m, n), F32),
        scratch_shapes=[pltpu.VMEM((tm, d), BF16)],
        compiler_params=_cparams(2),
        name="proj_gate_rows",
    )(x2, gn.reshape(1, d), wt)


def _gelu_tanh(x):
    return 0.5 * x * (1.0 + jnp.tanh(0.7978845608028654 * (x + 0.044715 * x * x * x)))


def _compress_blocks(load_rows, n_cmp, w1_ref, pos_ref, w2_ref, gk_ref):
    out = []
    for kv in range(2):
        sub = jnp.concatenate([load_rows(kv, j) for j in range(CMP_STRIDE)], axis=1).astype(BF16)
        hcat = jnp.dot(sub, w1_ref[kv], preferred_element_type=F32)
        bias = jnp.dot(pos_ref[kv], w1_ref[kv], preferred_element_type=F32)
        acc = None
        for g in range(NSA_KV_GROUPS):
            c0 = g * 2 * CMP_HIDDEN
            lo, hi = hcat[:, c0:c0 + CMP_HIDDEN], hcat[:, c0 + CMP_HIDDEN:c0 + 2 * CMP_HIDDEN]
            b = bias[0:1, c0:c0 + CMP_HIDDEN] + bias[1:2, c0 + CMP_HIDDEN:c0 + 2 * CMP_HIDDEN]
            hidden = lo + pltpu.roll(hi, n_cmp - 1, axis=0) + b
            part = jnp.dot(_gelu_tanh(hidden).astype(BF16), w2_ref[kv * NSA_KV_GROUPS + g],
                           preferred_element_type=F32)
            acc = part if acc is None else acc + part
        out.append(acc.T)
    parts = []
    for g in range(NSA_KV_GROUPS):
        blk = out[0][g * HEAD_DIM:(g + 1) * HEAD_DIM, :]
        ms = jnp.sum(blk * blk, axis=0, keepdims=True) * (1.0 / HEAD_DIM)
        parts.append(blk * lax.rsqrt(ms + NORM_EPS) * gk_ref[...])
    return jnp.concatenate(parts, axis=0), out[1]


def _compress_kernel(xk_ref, xv_ref, w1_ref, pos_ref, w2_ref, gk_ref, kc_ref, vct_ref):
    n_cmp = xk_ref.shape[1] // CMP_STRIDE
    x_refs = (xk_ref, xv_ref)
    kct, vct = _compress_blocks(lambda kv, j: x_refs[kv][0, pl.ds(j, n_cmp, stride=CMP_STRIDE), :],
                                n_cmp, w1_ref, pos_ref, w2_ref, gk_ref)
    kc_ref[0] = kct.T.astype(BF16)
    vct_ref[0] = vct.astype(BF16)


def _compress(c_rows, w1kv, poskv, w2pad, gk):
    nb, t, _ = c_rows.shape
    n_cmp = t // CMP_STRIDE
    full = lambda a: pl.BlockSpec(a.shape, lambda i: (0,) * a.ndim)
    return pl.pallas_call(
        _compress_kernel,
        grid=(nb,),
        in_specs=[pl.BlockSpec((1, t, NSA_KV_WIDTH), lambda i: (i, 0, 0)),
                  pl.BlockSpec((1, t, NSA_KV_WIDTH), lambda i: (i, 0, 1)),
                  full(w1kv), full(poskv), full(w2pad),
                  pl.BlockSpec((HEAD_DIM, 1), lambda i: (0, 0))],
        out_specs=[pl.BlockSpec((1, n_cmp, NSA_KV_WIDTH), lambda i: (i, 0, 0)),
                   pl.BlockSpec((1, NSA_KV_WIDTH, n_cmp), lambda i: (i, 0, 0))],
        out_shape=[jax.ShapeDtypeStruct((nb, n_cmp, NSA_KV_WIDTH), BF16),
                   jax.ShapeDtypeStruct((nb, NSA_KV_WIDTH, n_cmp), BF16)],
        compiler_params=_cparams(1),
        name="compress",
    )(c_rows, c_rows, w1kv, poskv, w2pad, gk.reshape(HEAD_DIM, 1))


M_FLOOR = -1e29


def _softmax_step(s, m, acc, vt_aug):
    m_new = jnp.maximum(m, jnp.max(s, axis=0, keepdims=True))
    p = jnp.exp(s - m_new).astype(BF16)
    acc = jnp.exp(m - m_new) * acc + jnp.dot(vt_aug, p, preferred_element_type=F32)
    return m_new, acc


def _rank_rows(score, blk):
    rank = jnp.zeros(score.shape, F32)
    for j in range(score.shape[0]):
        sj = score[j:j + 1, :]
        beats = (sj > score) | ((sj == score) & (blk > j))
        rank = rank + beats.astype(F32)
    return rank


def _rep_rows(row8, n_rows):
    return jnp.concatenate([row8] * (n_rows // 8), axis=0)


def _nsa_kernel(qn_ref, qr_ref, ga_ref, kc_ref, vct_ref, ks_ref, vst_ref, kw_ref, vwt_ref,
                o_ref, sel_ref, *, tq, tk, n_sel):
    qi = pl.program_id(1)
    q0 = qi * tq
    n = NSA_REP * tq
    n_cmp = kc_ref.shape[1]
    blocks_per_tile = tk // SEL_BLOCK
    lane = lax.broadcasted_iota(jnp.int32, (1, n), 1)
    t_lane = q0 + (lane & (tq - 1))
    t_q = q0 + lax.broadcasted_iota(jnp.int32, (1, tq), 1)
    cur = t_q >> 6
    blk = lax.broadcasted_iota(jnp.int32, (n_sel, 1), 0)
    cend = lax.broadcasted_iota(jnp.int32, (n_cmp, 1), 0) * CMP_STRIDE + (CMP_LEN - 1)
    krow = lax.broadcasted_iota(jnp.int32, (tk, 1), 0)
    ci = lax.broadcasted_iota(jnp.int32, (n_sel, n_cmp), 1)
    mi = lax.broadcasted_iota(jnp.int32, (n_sel, n_cmp), 0) * (SEL_BLOCK // CMP_STRIDE)
    amat = ((ci >= mi) & (ci <= mi + 3)).astype(F32) + ((ci >= mi - 1) & (ci <= mi + 2)).astype(F32)
    zeros_q = jnp.zeros((HEAD_DIM, n), BF16)
    ones_v = jnp.ones((HEAD_DIM, tk), BF16)

    def q_pad(ref, g):
        q = jnp.concatenate([ref[0, (g * NSA_REP + r) * HEAD_DIM:(g * NSA_REP + r + 1) * HEAD_DIM, :]
                             for r in range(NSA_REP)], axis=1) * SCALE
        return jnp.concatenate([q, zeros_q] if g == 0 else [zeros_q, q], axis=0)

    def gate_row(branch, g):
        return jnp.concatenate([ga_ref[0, branch * NSA_HEADS + g * NSA_REP + r:
                                       branch * NSA_HEADS + g * NSA_REP + r + 1, :]
                                for r in range(NSA_REP)], axis=1)

    for g in range(NSA_KV_GROUPS):
        rows = slice(g * HEAD_DIM, (g + 1) * HEAD_DIM)
        qn = q_pad(qn_ref, g)
        qr = q_pad(qr_ref, g)
        sc = jnp.dot(kc_ref[0], qn, preferred_element_type=F32)
        mask_c = cend <= t_lane
        smc = jnp.where(mask_c, sc, NEG)
        e = jnp.where(mask_c, jnp.exp(smc - jnp.max(smc, axis=0, keepdims=True)), 0.0)
        den = jnp.sum(e, axis=0, keepdims=True)
        pc = e / jnp.where(den > 0.0, den, 1.0)
        o_c = jnp.dot(vct_ref[0], pc.astype(BF16), preferred_element_type=F32)[rows, :]
        imp_n = jnp.dot(amat, pc, preferred_element_type=F32, precision=lax.Precision.HIGHEST)
        imp = imp_n[:, 0:tq]
        for r in range(1, NSA_REP):
            imp = imp + imp_n[:, r * tq:(r + 1) * tq]
        forced = (blk == 0) | (blk >= cur - 1)
        score = jnp.where(blk > cur, NEG, jnp.where(forced, FORCED, imp))
        sel = (_rank_rows(score, blk) < SEL_TOPK) & (blk <= cur)
        sel_n = jnp.concatenate([jnp.where(sel, 0.0, NEG)] * NSA_REP, axis=1)
        for m_ in range(n_sel):
            sel_ref[m_] = jnp.broadcast_to(sel_n[m_:m_ + 1, :], (8, n))

        def sel_tile(kv, carry, causal):
            m, acc = carry
            kt = ks_ref[0, pl.ds(pl.multiple_of(kv * tk, tk), tk), :]
            s = jnp.dot(kt, qr, preferred_element_type=F32) + jnp.concatenate(
                [_rep_rows(sel_ref[kv * blocks_per_tile + b_], SEL_BLOCK) for b_ in range(blocks_per_tile)],
                axis=0)
            if causal:
                s = jnp.where((kv * tk + krow) <= t_lane, s, NEG)
            return _softmax_step(s, m, acc, jnp.concatenate([vst_ref[0, kv, rows, :], ones_v], axis=0))

        init = (jnp.full((1, n), M_FLOOR, F32), jnp.zeros((2 * HEAD_DIM, n), F32))
        last = (q0 + tq - 1) // tk
        carry = lax.fori_loop(0, last, lambda kv, c: sel_tile(kv, c, False), init)
        _, acc_s = sel_tile(last, carry, True)
        o_s = acc_s[:HEAD_DIM] / acc_s[HEAD_DIM:HEAD_DIM + 1]

        def win_tile(kv, carry, newest):
            m, acc = carry
            kt = kw_ref[0, pl.ds(pl.multiple_of(kv * tk, tk), tk), :]
            rel = t_lane - (kv * tk + krow)
            s = jnp.where((rel >= 0) if newest else (rel < WINDOW),
                          jnp.dot(kt, qr, preferred_element_type=F32), NEG)
            return _softmax_step(s, m, acc, jnp.concatenate([vwt_ref[0, kv, rows, :], ones_v], axis=0))

        carry = lax.fori_loop(jnp.maximum(last - 1, 0), last, lambda kv, c: win_tile(kv, c, False), init)
        _, acc_w = win_tile(last, carry, True)
        o_w = acc_w[:HEAD_DIM] / acc_w[HEAD_DIM:HEAD_DIM + 1]

        o = gate_row(0, g) * o_c + gate_row(1, g) * o_s + gate_row(2, g) * o_w
        o_heads = jnp.concatenate([o[:, r * tq:(r + 1) * tq] for r in range(NSA_REP)], axis=0)
        o_ref[0, :, g * NSA_REP * HEAD_DIM:(g + 1) * NSA_REP * HEAD_DIM] = o_heads.T.astype(BF16)


def _nsa(qn_t, qr_t, ga_t, kc, vc_t, k_rows, ks_col, vs_tt, vs_row, kw_rows, kw_col, vw_tt, vw_row):
    b, _, tq_total = qn_t.shape
    tq, tk = NSA_Q_TILE, KV_TILE
    assert WINDOW == tk and tk % tq == 0 and tq_total % tk == 0
    n_cmp = kc.shape[1]
    l_sel = k_rows.shape[1]
    n_sel = ((l_sel // SEL_BLOCK) + 7) // 8 * 8
    l_win = kw_rows.shape[1]
    n = NSA_REP * tq
    return pl.pallas_call(
        functools.partial(_nsa_kernel, tq=tq, tk=tk, n_sel=n_sel),
        grid=(b, tq_total // tq),
        in_specs=[
            pl.BlockSpec((1, NSA_WIDTH, tq), lambda bb, i: (bb, 0, i)),
            pl.BlockSpec((1, NSA_WIDTH, tq), lambda bb, i: (bb, 0, i)),
            pl.BlockSpec((1, ga_t.shape[1], tq), lambda bb, i: (bb, 0, i)),
            pl.BlockSpec((1, n_cmp, NSA_KV_WIDTH), lambda bb, i: (bb, 0, 0)),
            pl.BlockSpec((1, NSA_KV_WIDTH, n_cmp), lambda bb, i: (bb, 0, 0)),
            pl.BlockSpec((1, l_sel, NSA_KV_WIDTH), lambda bb, i: (bb, 0, ks_col)),
            pl.BlockSpec((1, l_sel // tk, NSA_KV_WIDTH, tk), lambda bb, i: (bb, 0, vs_row, 0)),
            pl.BlockSpec((1, l_win, NSA_KV_WIDTH), lambda bb, i: (bb, 0, kw_col)),
            pl.BlockSpec((1, l_win // tk, NSA_KV_WIDTH, tk), lambda bb, i: (bb, 0, vw_row, 0)),
        ],
        out_specs=pl.BlockSpec((1, tq, NSA_WIDTH), lambda bb, i: (bb, i, 0)),
        out_shape=jax.ShapeDtypeStruct((b, tq_total, NSA_WIDTH), BF16),
        scratch_shapes=[pltpu.VMEM((n_sel, 8, n), F32)],
        compiler_params=_cparams(2),
        name="nsa_attend",
    )(qn_t, qr_t, ga_t, kc, vc_t, k_rows, vs_tt, kw_rows, vw_tt)


def _moba_kernel(q_ref, k_ref, vt_ref, o_ref, km_ref, sel_ref, *, tq, tk, n_blk):
    qi = pl.program_id(2)
    q0 = qi * tq
    n = 2 * tq
    cq = q0 // MOBA_BLOCK
    blocks_per_tile = tk // MOBA_BLOCK
    n_rows = km_ref.shape[0]

    @pl.when(qi == 0)
    def _():
        kall = k_ref[0].astype(F32)
        km = jnp.sum(kall.reshape(n_blk, MOBA_BLOCK, 2 * HEAD_DIM), axis=1) * (1.0 / MOBA_BLOCK)
        if n_rows > n_blk:
            km = jnp.concatenate([km, jnp.zeros((n_rows - n_blk, 2 * HEAD_DIM), F32)], axis=0)
        km_ref[...] = km.astype(BF16)

    lane = lax.broadcasted_iota(jnp.int32, (1, n), 1)
    t_lane = q0 + (lane & (tq - 1))
    blk = lax.broadcasted_iota(jnp.int32, (n_rows, 1), 0)
    krow = lax.broadcasted_iota(jnp.int32, (MOBA_BLOCK, 1), 0)
    zq = jnp.zeros((HEAD_DIM, tq), BF16)
    ones_v = jnp.ones((16, tk), BF16)
    qbd = jnp.concatenate([jnp.concatenate([q_ref[0, 0:HEAD_DIM, :] * SCALE, zq], axis=0),
                           jnp.concatenate([zq, q_ref[0, HEAD_DIM:2 * HEAD_DIM, :] * SCALE], axis=0)], axis=1)
    s_blk = jnp.dot(km_ref[...], qbd, preferred_element_type=F32) * (1.0 / SCALE)
    past = blk < cq
    score = jnp.where(past, s_blk, NEG)
    sel = (_rank_rows(score, blk) < MOBA_TOPK) & past
    sel_b = jnp.where(sel, 0.0, NEG)
    for m_ in range(n_rows):
        sel_ref[m_] = jnp.broadcast_to(sel_b[m_:m_ + 1, :], (8, n))

    def tile(kv, carry, own_tile):
        m, acc = carry
        kt = k_ref[0, pl.ds(pl.multiple_of(kv * tk, tk), tk), :]
        s = jnp.dot(kt, qbd, preferred_element_type=F32)
        parts = []
        for b_ in range(blocks_per_tile):
            nb = kv * blocks_per_tile + b_
            sb = s[b_ * MOBA_BLOCK:(b_ + 1) * MOBA_BLOCK, :]
            picked = sb + _rep_rows(sel_ref[nb], MOBA_BLOCK)
            if own_tile:
                own = jnp.where((nb * MOBA_BLOCK + krow) <= t_lane, sb, NEG)
                picked = jnp.where(nb == cq, own, picked)
            parts.append(picked)
        vt_aug = jnp.concatenate([vt_ref[0, kv], ones_v], axis=0)
        return _softmax_step(jnp.concatenate(parts, axis=0), m, acc, vt_aug)

    init = (jnp.full((1, n), M_FLOOR, F32), jnp.zeros((2 * HEAD_DIM + 16, n), F32))
    last = (q0 + tq - 1) // tk
    carry = lax.fori_loop(0, last, lambda kv, c: tile(kv, c, False), init)
    _, acc = tile(last, carry, True)
    o = acc[:2 * HEAD_DIM] / acc[2 * HEAD_DIM:2 * HEAD_DIM + 1]
    o2 = jnp.concatenate([o[0:HEAD_DIM, 0:tq], o[HEAD_DIM:2 * HEAD_DIM, tq:2 * tq]], axis=0)
    o_ref[0] = o2.T.astype(BF16)


def _moba(q_t, k_rows, k_col0, v_tt, v_row0):
    b, _, tq_total = q_t.shape
    tq, tk = MOBA_Q_TILE, KV_TILE
    assert tq == MOBA_BLOCK and tk % tq == 0
    l_kv = k_rows.shape[1]
    n_blk = l_kv // MOBA_BLOCK
    n_rows = (n_blk + 7) // 8 * 8
    n = 2 * tq
    pair = 2 * HEAD_DIM
    return pl.pallas_call(
        functools.partial(_moba_kernel, tq=tq, tk=tk, n_blk=n_blk),
        grid=(b, MOBA_HEADS // 2, tq_total // tq),
        in_specs=[
            pl.BlockSpec((1, pair, tq), lambda bb, hp, i: (bb, hp, i)),
            pl.BlockSpec((1, l_kv, pair), lambda bb, hp, i: (bb, 0, k_col0 + hp)),
            pl.BlockSpec((1, l_kv // tk, pair, tk), lambda bb, hp, i: (bb, 0, v_row0 + hp, 0)),
        ],
        out_specs=pl.BlockSpec((1, tq, pair), lambda bb, hp, i: (bb, i, hp)),
        out_shape=jax.ShapeDtypeStruct((b, tq_total, MOBA_WIDTH), BF16),
        scratch_shapes=[pltpu.VMEM((n_rows, pair), BF16), pltpu.VMEM((n_rows, 8, n), F32)],
        compiler_params=_cparams(3),
        name="moba_attend",
    )(q_t, k_rows, v_tt)


def _merge_kernel(x_ref, oa_ref, ob_ref, gm_ref, wa_ref, wb_ref, wo_ref, o_ref):
    ua = jnp.dot(oa_ref[...], wa_ref[...], preferred_element_type=F32)
    ub = jnp.dot(ob_ref[...], wb_ref[...], preferred_element_type=F32)
    mixed = gm_ref[:, :D_MODEL] * ua + gm_ref[:, D_MODEL:] * ub
    o_ref[...] = x_ref[...] + jnp.dot(mixed.astype(BF16), wo_ref[...], preferred_element_type=F32)


def _merge(x2, oa, ob, gm, wa, wb, wo, *, tm):
    m, d = x2.shape
    row = lambda i: (i, 0)
    fixed = lambda i: (0, 0)
    return pl.pallas_call(
        _merge_kernel,
        grid=(m // tm,),
        in_specs=[pl.BlockSpec((tm, d), row), pl.BlockSpec((tm, NSA_WIDTH), row),
                  pl.BlockSpec((tm, MOBA_WIDTH), row), pl.BlockSpec((tm, 2 * d), row),
                  pl.BlockSpec(wa.shape, fixed), pl.BlockSpec(wb.shape, fixed), pl.BlockSpec(wo.shape, fixed)],
        out_specs=pl.BlockSpec((tm, d), row),
        out_shape=jax.ShapeDtypeStruct((m, d), F32),
        compiler_params=_cparams(1),
        name="merge_out",
    )(x2, oa, ob, gm, wa, wb, wo)


def _mlp_kernel(x_ref, g_ref, w1_ref, w2_ref, o_ref, h_ref, acc_ref):
    c = pl.program_id(1)

    @pl.when(c == 0)
    def _():
        h_ref[...] = _rms_rows(x_ref[...], g_ref[...]).astype(BF16)
        acc_ref[...] = x_ref[...]

    u = jnp.maximum(jnp.dot(h_ref[...], w1_ref[...], preferred_element_type=F32), 0.0)
    acc_ref[...] += jnp.dot((u * u).astype(BF16), w2_ref[...], preferred_element_type=F32)

    @pl.when(c == pl.num_programs(1) - 1)
    def _():
        o_ref[...] = acc_ref[...]


def _mlp(x2, g, w1, w2, *, tm, tf):
    m, d = x2.shape
    dff = w1.shape[1]
    return pl.pallas_call(
        _mlp_kernel,
        grid=(m // tm, dff // tf),
        in_specs=[pl.BlockSpec((tm, d), lambda i, c: (i, 0)),
                  pl.BlockSpec((1, d), lambda i, c: (0, 0)),
                  pl.BlockSpec((d, tf), lambda i, c: (0, c)),
                  pl.BlockSpec((tf, d), lambda i, c: (c, 0))],
        out_specs=pl.BlockSpec((tm, d), lambda i, c: (i, 0)),
        out_shape=jax.ShapeDtypeStruct((m, d), F32),
        scratch_shapes=[pltpu.VMEM((tm, d), BF16), pltpu.VMEM((tm, d), F32)],
        compiler_params=_cparams(2),
        name="sq_relu_mlp",
    )(x2, g.reshape(1, d), w1, w2)


LANES = 128


def _stack_rows(rows, n_rows=8):
    idx = lax.broadcasted_iota(jnp.int32, (n_rows, 1), 0)
    out = jnp.zeros((n_rows, rows[0].shape[1]), F32)
    for i, r in enumerate(rows):
        out = jnp.where(idx == i, r, out)
    return out


def _lane_col(x, j, lane):
    return jnp.sum(jnp.where(lane == j, x, 0.0), axis=1, keepdims=True)


def _col_tiles(row):
    m = jnp.broadcast_to(row, (LANES, LANES)).T
    return [m[:HEAD_DIM], m[HEAD_DIM:]]


def _row_of_cols(cols):
    m = jnp.broadcast_to(jnp.concatenate(cols, axis=0), (LANES, LANES)).T
    return m[0:1, :]


def _attend_pages(kts, vts, masks, q_tiles, k_new, v_new, new_ok, lane):
    outs = []
    for qh in q_tiles:
        s = [jnp.where(masks[p], jnp.sum(kts[p] * qh, axis=0, keepdims=True) * SCALE, NEG)
             for p in range(len(kts))]
        s_new = jnp.where(new_ok, jnp.sum(k_new * qh, axis=0, keepdims=True) * SCALE, NEG)
        m = s_new
        for sp in s:
            m = jnp.maximum(m, sp)
        m = jnp.max(m, axis=1, keepdims=True)
        e_new = jnp.where(new_ok & (lane == 0), jnp.exp(s_new - m), 0.0)
        l = e_new
        acc = e_new * v_new
        for p in range(len(kts)):
            e = jnp.where(masks[p], jnp.exp(s[p] - m), 0.0)
            l = l + e
            acc = acc + vts[p] * e
        outs.append(jnp.sum(acc, axis=1, keepdims=True) / jnp.sum(l, axis=1, keepdims=True))
    return outs


def _decode_kernel(pt_ref, *refs, n_pages, t_pos, w_buf):
    del pt_ref
    cmp_pages = refs[:n_pages]
    sel_pages = refs[n_pages:2 * n_pages]
    moba_pages = refs[2 * n_pages:3 * n_pages]
    (win_ref, q_ref, new_ref, ga_ref, w1_ref, pos_ref, w2_ref, gk_ref,
     oa_ref, ob_ref, xk_ref, xv_ref) = refs[3 * n_pages:]
    past = n_pages * PAGE_SIZE
    n_cmp = past // CMP_STRIDE
    lane = lax.broadcasted_iota(jnp.int32, (1, LANES), 1)
    sub_i = lax.broadcasted_iota(jnp.int32, (LANES, 1), 0)
    row8 = lax.broadcasted_iota(jnp.int32, (8, 1), 0)

    def tiles(ref, off, n_feat):
        out = []
        for c in range(n_feat // LANES):
            out += _col_tiles(ref[0, :, off + c * LANES:off + (c + 1) * LANES])
        return out

    qn, qr, qb = tiles(q_ref, 0, NSA_WIDTH), tiles(q_ref, NSA_WIDTH, NSA_WIDTH), tiles(q_ref, 2 * NSA_WIDTH, MOBA_WIDTH)
    kvw = NSA_KV_WIDTH
    ks_new, vs_new = tiles(new_ref, 0, kvw), tiles(new_ref, kvw, kvw)
    kw_new, vw_new = tiles(new_ref, 2 * kvw, kvw), tiles(new_ref, 3 * kvw, kvw)
    kb_new, vb_new = tiles(new_ref, 4 * kvw, MOBA_WIDTH), tiles(new_ref, 4 * kvw + MOBA_WIDTH, MOBA_WIDTH)
    ga_row = ga_ref[0]

    x_refs = (xk_ref, xv_ref)
    for p in range(n_pages):
        for kv in range(2):
            x_refs[kv][p * PAGE_SIZE:(p + 1) * PAGE_SIZE, :] = cmp_pages[p][kv].reshape(kvw, PAGE_SIZE).T
    ck_t = _compress_blocks(lambda kv, j: x_refs[kv][pl.ds(j, n_cmp, stride=CMP_STRIDE), :],
                            n_cmp, w1_ref, pos_ref, w2_ref, gk_ref)

    cend_ok = (lane * CMP_STRIDE + (CMP_LEN - 1)) <= t_pos
    cur = t_pos // SEL_BLOCK
    forced = (lane == 0) | (lane >= cur - 1)
    mi = lane * (SEL_BLOCK // CMP_STRIDE)
    amat_t = (((sub_i >= mi) & (sub_i <= mi + 3)).astype(F32)
              + ((sub_i >= mi - 1) & (sub_i <= mi + 2)).astype(F32))
    blocks_per_page = PAGE_SIZE // SEL_BLOCK

    def gate(branch, h):
        return _lane_col(ga_row, branch * NSA_HEADS + h, lane)

    nsa_cols = []
    for g in range(NSA_KV_GROUPS):
        rows = slice(g * HEAD_DIM, (g + 1) * HEAD_DIM)
        heads = range(g * NSA_REP, (g + 1) * NSA_REP)
        kct, vct = ck_t[0][rows, :], ck_t[1][rows, :]
        s4 = _stack_rows([jnp.sum(kct * qn[h], axis=0, keepdims=True) * SCALE for h in heads])
        mask4 = cend_ok & (row8 < NSA_REP)
        sm = jnp.where(mask4, s4, NEG)
        e = jnp.where(mask4, jnp.exp(sm - jnp.max(sm, axis=1, keepdims=True)), 0.0)
        den = jnp.sum(e, axis=1, keepdims=True)
        pc = e / jnp.where(den > 0.0, den, 1.0)
        o_c = [jnp.sum(vct * pc[r:r + 1, :], axis=1, keepdims=True) for r in range(NSA_REP)]
        psum = jnp.broadcast_to(jnp.sum(pc, axis=0, keepdims=True), (8, LANES))
        imp = jnp.dot(psum, amat_t, preferred_element_type=F32, precision=lax.Precision.HIGHEST)[0:1, :]
        score = jnp.where(lane > cur, NEG, jnp.where(forced, FORCED, imp))
        rm = jnp.broadcast_to(score, (LANES, LANES))
        cm = rm.T
        rank = jnp.sum(((cm > rm) | ((cm == rm) & (sub_i < lane))).astype(F32), axis=0, keepdims=True)
        sel = ((rank < SEL_TOPK) & (lane <= cur)).astype(F32)
        masks = []
        for p in range(n_pages):
            picks = [_lane_col(sel, p * blocks_per_page + b_, lane) for b_ in range(blocks_per_page)]
            mrow = picks[-1]
            for b_ in range(blocks_per_page - 2, -1, -1):
                mrow = jnp.where(lane < (b_ + 1) * SEL_BLOCK, picks[b_], mrow)
            masks.append(mrow > 0.5)
        new_ok = _lane_col(sel, cur, lane) > 0.5
        o_s = _attend_pages([sel_pages[p][0, g] for p in range(n_pages)],
                            [sel_pages[p][1, g] for p in range(n_pages)],
                            masks, [qr[h] for h in heads], ks_new[g], vs_new[g], new_ok, lane)
        wmasks = []
        for c in range(w_buf // LANES):
            rel = t_pos - (past - w_buf + c * LANES + lane)
            wmasks.append((rel >= 0) & (rel < WINDOW))
        o_w = _attend_pages([win_ref[0, g, :, c * LANES:(c + 1) * LANES] for c in range(w_buf // LANES)],
                            [win_ref[1, g, :, c * LANES:(c + 1) * LANES] for c in range(w_buf // LANES)],
                            wmasks, [qr[h] for h in heads], kw_new[g], vw_new[g], lane >= 0, lane)
        for r, h in enumerate(heads):
            nsa_cols.append(gate(0, h) * o_c[r] + gate(1, h) * o_s[r] + gate(2, h) * o_w[r])
    for c in range(NSA_HEADS // 2):
        oa_ref[0, :, c * LANES:(c + 1) * LANES] = _row_of_cols(nsa_cols[2 * c:2 * c + 2])

    pages_per_blk = MOBA_BLOCK // PAGE_SIZE
    cq = t_pos // MOBA_BLOCK
    raw = [_stack_rows([jnp.sum(moba_pages[p][0, h] * qb[h], axis=0, keepdims=True) for h in range(MOBA_HEADS)])
           for p in range(n_pages)]
    s_blk = jnp.zeros((MOBA_HEADS, LANES), F32)
    for n_ in range(past // MOBA_BLOCK):
        tot = raw[n_ * pages_per_blk]
        for k_ in range(1, pages_per_blk):
            tot = tot + raw[n_ * pages_per_blk + k_]
        s_blk = jnp.where(lane == n_, jnp.sum(tot, axis=1, keepdims=True) * (1.0 / MOBA_BLOCK), s_blk)
    past_m = lane < cq
    score = jnp.where(past_m, s_blk, NEG)
    rank = jnp.zeros((MOBA_HEADS, LANES), F32)
    for j in range(cq):
        cj = _lane_col(score, j, lane)
        rank = rank + ((cj > score) | ((cj == score) & (lane > j))).astype(F32)
    sel = (past_m & (rank < MOBA_TOPK)).astype(F32)
    picked = [_lane_col(sel, n_, lane) > 0.5 for n_ in range(past // MOBA_BLOCK)]
    s_new = _stack_rows([jnp.sum(kb_new[h] * qb[h], axis=0, keepdims=True) for h in range(MOBA_HEADS)]) * SCALE
    sc = [jnp.where(picked[p // pages_per_blk], raw[p] * SCALE, NEG) for p in range(n_pages)]
    m = s_new
    for sp in sc:
        m = jnp.maximum(m, sp)
    m = jnp.max(m, axis=1, keepdims=True)
    e_new = jnp.where(lane == 0, jnp.exp(s_new - m), 0.0)
    es = [jnp.where(picked[p // pages_per_blk], jnp.exp(sc[p] - m), 0.0) for p in range(n_pages)]
    l = e_new
    for e in es:
        l = l + e
    den = jnp.sum(l, axis=1, keepdims=True)
    moba_cols = []
    for h in range(MOBA_HEADS):
        acc = e_new[h:h + 1, :] * vb_new[h]
        for p in range(n_pages):
            acc = acc + moba_pages[p][1, h] * es[p][h:h + 1, :]
        moba_cols.append(jnp.sum(acc, axis=1, keepdims=True) / den[h:h + 1, :])
    for c in range(MOBA_HEADS // 2):
        ob_ref[0, :, c * LANES:(c + 1) * LANES] = _row_of_cols(moba_cols[2 * c:2 * c + 2])


def _decode(cmp_t, sel_t, moba_t, win_t, layer, page_table, q_rows, new_rows, ga_rows, w1kv, poskv, w2pad, gk):
    n_seq, n_pages = page_table.shape
    past = n_pages * PAGE_SIZE
    w_buf = win_t.shape[-1]
    assert past // CMP_STRIDE == LANES and past % MOBA_BLOCK == 0 and w_buf % LANES == 0

    def page_spec(h, p):
        return pl.BlockSpec((None, None, 2, h, HEAD_DIM, PAGE_SIZE), lambda s, pt: (layer, pt[s, p], 0, 0, 0, 0))

    row_spec = lambda a: pl.BlockSpec((1, 1, a.shape[2]), lambda s, pt: (s, 0, 0))
    full = lambda a: pl.BlockSpec(a.shape, lambda s, pt: (0,) * a.ndim)
    in_specs = ([page_spec(NSA_KV_GROUPS, p) for p in range(n_pages)]
                + [page_spec(NSA_KV_GROUPS, p) for p in range(n_pages)]
                + [page_spec(MOBA_HEADS, p) for p in range(n_pages)]
                + [pl.BlockSpec((None, None, 2, NSA_KV_GROUPS, HEAD_DIM, w_buf), lambda s, pt: (layer, s, 0, 0, 0, 0)),
                   row_spec(q_rows), row_spec(new_rows), row_spec(ga_rows),
                   full(w1kv), full(poskv), full(w2pad), pl.BlockSpec((HEAD_DIM, 1), lambda s, pt: (0, 0))])
    out_spec = pl.BlockSpec((1, 1, NSA_WIDTH), lambda s, pt: (s, 0, 0))
    return pl.pallas_call(
        functools.partial(_decode_kernel, n_pages=n_pages, t_pos=past, w_buf=w_buf),
        grid_spec=pltpu.PrefetchScalarGridSpec(
            num_scalar_prefetch=1, grid=(n_seq,), in_specs=in_specs, out_specs=[out_spec, out_spec],
            scratch_shapes=[pltpu.VMEM((past, NSA_KV_WIDTH), F32)] * 2),
        out_shape=[jax.ShapeDtypeStruct((n_seq, 1, NSA_WIDTH), F32)] * 2,
        compiler_params=_cparams(1),
        name="decode_mixers",
    )(page_table, *([cmp_t] * n_pages), *([sel_t] * n_pages), *([moba_t] * n_pages), win_t,
      q_rows, new_rows, ga_rows, w1kv, poskv, w2pad, gk.reshape(HEAD_DIM, 1))


_SPLITS = (NSA_WIDTH, NSA_KV_WIDTH, NSA_KV_WIDTH, NSA_KV_WIDTH, NSA_KV_WIDTH, NSA_KV_WIDTH, NSA_KV_WIDTH,
           3 * NSA_HEADS, MOBA_WIDTH, MOBA_WIDTH, MOBA_WIDTH, 2 * D_MODEL)
_NAMES = ("qa", "kc", "vc", "ks", "vs", "kw", "vw", "ga", "qb", "kb", "vb", "gm")


def _layer_params(l, w_in, gq_nsa, gk_sel, gk_win, gq_moba, gk_moba, cmp_pos_k, cmp_pos_v,
                  cmp_w1_k, cmp_w2_k, cmp_w1_v, cmp_w2_v):
    wt = w_in[l].T
    off, part = 0, {}
    for name, size in zip(_NAMES, _SPLITS):
        part[name] = wt[off:off + size]
        off += size
    cat = lambda *names: jnp.concatenate([part[n_] for n_ in names], axis=0).astype(BF16)
    tile = lambda g_, reps: jnp.tile(g_, reps)
    ga_pad = jnp.concatenate([part["ga"], jnp.zeros((HEAD_DIM - 3 * NSA_HEADS, D_MODEL), F32)], axis=0)
    p = {
        "w_q": cat("qa"), "g_q": tile(gq_nsa[l], NSA_HEADS),
        "w_kr": cat("ks", "kw", "kb"),
        "g_kr": jnp.concatenate([tile(gk_sel[l], NSA_KV_GROUPS), tile(gk_win[l], NSA_KV_GROUPS),
                                 tile(gk_moba[l], MOBA_HEADS)]),
        "w_qb": cat("qb"), "g_qb": tile(gq_moba[l], MOBA_HEADS),
        "w_c": cat("kc", "vc"),
        "w_v": cat("vs", "vw", "vb"),
        "w_ga": ga_pad.astype(BF16),
        "w_gm": cat("gm"),
    }
    w1kv, poskv, w2pad = [], [], []
    for w1, w2, pos in ((cmp_w1_k[l], cmp_w2_k[l], cmp_pos_k[l]), (cmp_w1_v[l], cmp_w2_v[l], cmp_pos_v[l])):
        w1r = w1.reshape(2, CMP_STRIDE, HEAD_DIM, CMP_HIDDEN)
        lohi = jnp.concatenate([w1r[0], w1r[1]], axis=-1)
        z = jnp.zeros_like(lohi)
        w1kv.append(jnp.stack([jnp.concatenate([lohi, z], axis=-1), jnp.concatenate([z, lohi], axis=-1)], axis=1)
                    .reshape(CMP_STRIDE * NSA_KV_WIDTH, NSA_KV_GROUPS * 2 * CMP_HIDDEN))
        posr = jnp.tile(pos.reshape(2, CMP_STRIDE, 1, HEAD_DIM), (1, 1, NSA_KV_GROUPS, 1))
        poskv.append(jnp.pad(posr.reshape(2, CMP_STRIDE * NSA_KV_WIDTH), ((0, 6), (0, 0))))
        zw = jnp.zeros_like(w2)
        w2pad += [jnp.concatenate([w2, zw], axis=1), jnp.concatenate([zw, w2], axis=1)]
    p["w1kv"] = jnp.stack(w1kv).astype(BF16)
    p["poskv"] = jnp.stack(poskv).astype(BF16)
    p["w2pad"] = jnp.stack(w2pad).astype(BF16)
    return p


def _rope_tables(pos):
    inv = ROPE_THETA ** (-jnp.arange(HALF, dtype=F32) / HALF)
    ang = inv[:, None] * pos.astype(F32)[None, :]
    return jnp.cos(ang), jnp.sin(ang)


def _project_all(x3, p, gn, cos, sin, tm, *, attention_copies):
    t_f32, t_bf, r_bf = ("final", "T", F32), ("final", "T", BF16), ("final", "R", BF16)
    extra = lambda o: (t_f32, o) if attention_copies else (t_f32,)
    out = {}
    out["qn_t"], out["qr_t"] = _proj_t(x3, gn, p["w_q"], tm=tm, tn=512, head_gain=p["g_q"], cos=cos, sin=sin,
                                       outs=(("normed", "T", BF16), t_bf))
    out["kr_t"], *rest = _proj_t(x3, gn, p["w_kr"], tm=tm, tn=256, head_gain=p["g_kr"], cos=cos, sin=sin,
                                 outs=extra(r_bf))
    out["kr_r"] = rest[0] if rest else None
    (out["qb_t"],) = _proj_t(x3, gn, p["w_qb"], tm=tm, tn=512, head_gain=p["g_qb"], cos=cos, sin=sin,
                             outs=(t_bf,))
    out["c_t"], *rest = _proj_t(x3, gn, p["w_c"], tm=tm, tn=256, outs=extra(("final", "R", F32)))
    out["c_r"] = rest[0] if rest else None
    out["v_t"], *rest = _proj_t(x3, gn, p["w_v"], tm=tm, tn=256, outs=extra(("final", "TT", BF16)))
    out["v_x"] = rest[0] if rest else None
    (out["ga_t"],) = _proj_t(x3, gn, p["w_ga"], tm=tm, tn=HEAD_DIM, act="sigmoid", outs=(t_f32,))
    b, t, d = x3.shape
    out["gm"] = _proj_rows_sigmoid(x3.reshape(b * t, d), gn, p["w_gm"], tm=tm, tn=512)
    return out


def _cache_leaf(feat_major, heads):
    b, _, t = feat_major.shape
    return feat_major.reshape(b, 2, heads, HEAD_DIM, t).transpose(0, 4, 1, 2, 3)


def kernel(x_prompt, x_sample, cache_nsa_cmp, cache_nsa_sel, cache_moba, state_nsa_win, page_table,
           norm_mix, w_in, gq_nsa, gk_cmp, gk_sel, gk_win, gq_moba, gk_moba,
           cmp_pos_k, cmp_pos_v, cmp_w1_k, cmp_w2_k, cmp_w1_v, cmp_w2_v,
           w_up_nsa, w_up_moba, w_out, norm_mlp, w_mlp_up, w_mlp_down):
    depth = w_in.shape[0]
    b, t, d = x_prompt.shape
    n_dec = x_sample.shape[0]
    past_len = page_table.shape[1] * PAGE_SIZE

    cos_p, sin_p = _rope_tables(jnp.arange(t, dtype=jnp.int32))
    cos_s, sin_s = _rope_tables(jnp.full((n_dec,), past_len, dtype=jnp.int32))
    to_t = lambda c: c.transpose(0, 1, 3, 4, 5, 2)
    cmp_t, sel_t, moba_t, win_t = to_t(cache_nsa_cmp), to_t(cache_nsa_sel), to_t(cache_moba), to_t(state_nsa_win)

    xp = x_prompt
    xs = x_sample.reshape(1, n_dec, d)
    leaves = [[] for _ in range(8)]
    tm_p = 512
    for l in range(depth):
        p = _layer_params(l, w_in, gq_nsa, gk_sel, gk_win, gq_moba, gk_moba, cmp_pos_k, cmp_pos_v,
                          cmp_w1_k, cmp_w2_k, cmp_w1_v, cmp_w2_v)
        wa, wb, wo = w_up_nsa[l].astype(BF16), w_up_moba[l].astype(BF16), w_out[l].astype(BF16)
        w1, w2 = w_mlp_up[l].astype(BF16), w_mlp_down[l].astype(BF16)

        pr = _project_all(xp, p, norm_mix[l], cos_p, sin_p, tm_p, attention_copies=True)
        kc, vc_t = _compress(pr["c_r"], p["w1kv"], p["poskv"], p["w2pad"], gk_cmp[l])
        oa = _nsa(pr["qn_t"], pr["qr_t"], pr["ga_t"], kc, vc_t,
                  pr["kr_r"], 0, pr["v_x"], 0, pr["kr_r"], 1, pr["v_x"], 1)
        ob = _moba(pr["qb_t"], pr["kr_r"], 2, pr["v_x"], 2)
        x2 = _merge(xp.reshape(b * t, d), oa.reshape(b * t, NSA_WIDTH), ob.reshape(b * t, MOBA_WIDTH),
                    pr["gm"], wa, wb, wo, tm=tm_p)
        xp = _mlp(x2, norm_mlp[l], w1, w2, tm=tm_p, tf=1024).reshape(b, t, d)
        kr_t, v_t = pr["kr_t"], pr["v_t"]
        kvw = NSA_KV_WIDTH
        leaves[0].append(_cache_leaf(pr["c_t"], NSA_KV_GROUPS))
        leaves[1].append(_cache_leaf(jnp.concatenate([kr_t[:, :kvw], v_t[:, :kvw]], axis=1), NSA_KV_GROUPS))
        leaves[2].append(_cache_leaf(jnp.concatenate([kr_t[:, 2 * kvw:], v_t[:, 2 * kvw:]], axis=1), MOBA_HEADS))
        w_keep = min(WINDOW, t)
        leaves[3].append(_cache_leaf(jnp.concatenate([kr_t[:, kvw:2 * kvw, t - w_keep:],
                                                      v_t[:, kvw:2 * kvw, t - w_keep:]], axis=1), NSA_KV_GROUPS))

        sr = _project_all(xs, p, norm_mix[l], cos_s, sin_s, n_dec, attention_copies=False)
        kr_new = sr["kr_t"][0].T
        c_new, v_new = sr["c_t"][0].T, sr["v_t"][0].T
        new_sel = jnp.concatenate([kr_new[:, :kvw], v_new[:, :kvw]], axis=1)
        new_win = jnp.concatenate([kr_new[:, kvw:2 * kvw], v_new[:, kvw:2 * kvw]], axis=1)
        new_moba = jnp.concatenate([kr_new[:, 2 * kvw:], v_new[:, 2 * kvw:]], axis=1)
        seq_rows = lambda a: a[0].T.astype(F32)
        q_rows = jnp.concatenate([seq_rows(sr["qn_t"]), seq_rows(sr["qr_t"]), seq_rows(sr["qb_t"])], axis=1)
        new_rows = jnp.concatenate([new_sel, new_win, new_moba], axis=1)
        ga_rows = jnp.pad(seq_rows(sr["ga_t"]), ((0, 0), (0, LANES - HEAD_DIM)))
        oa, ob = _decode(cmp_t, sel_t, moba_t, win_t, l, page_table, q_rows[:, None, :], new_rows[:, None, :],
                         ga_rows[:, None, :], p["w1kv"], p["poskv"], p["w2pad"], gk_cmp[l])
        x2 = _merge(xs.reshape(n_dec, d), oa[:, 0].astype(BF16), ob[:, 0].astype(BF16), sr["gm"], wa, wb, wo, tm=n_dec)
        xs = _mlp(x2, norm_mlp[l], w1, w2, tm=n_dec, tf=1024).reshape(1, n_dec, d)
        seq_leaf = lambda rows, heads: rows.reshape(n_dec, 1, 2, heads, HEAD_DIM)
        leaves[4].append(seq_leaf(c_new, NSA_KV_GROUPS))
        leaves[5].append(seq_leaf(new_sel, NSA_KV_GROUPS))
        leaves[6].append(seq_leaf(new_moba, MOBA_HEADS))
        win_all = jnp.concatenate([state_nsa_win[l], seq_leaf(new_win, NSA_KV_GROUPS)], axis=1)
        leaves[7].append(win_all[:, -min(WINDOW, past_len + 1):])
    return (xp, xs.reshape(n_dec, 1, d)) + tuple(jnp.stack(v) for v in leaves)
```

```python
import functools

import jax
import jax.numpy as jnp
from jax import lax
from jax.experimental import pallas as pl
from jax.experimental.pallas import tpu as pltpu

F32 = jnp.float32
BF16 = jnp.bfloat16

D_MODEL = 1024
HEAD_DIM = 64
HALF = HEAD_DIM // 2
NSA_HEADS = 8
NSA_KV_GROUPS = 2
NSA_REP = NSA_HEADS // NSA_KV_GROUPS
MOBA_HEADS = 8
NSA_WIDTH = NSA_HEADS * HEAD_DIM
NSA_KV_WIDTH = NSA_KV_GROUPS * HEAD_DIM
MOBA_WIDTH = MOBA_HEADS * HEAD_DIM
CMP_LEN = 32
CMP_STRIDE = 16
CMP_HIDDEN = 2 * HEAD_DIM
SEL_BLOCK = 64
SEL_TOPK = 16
WINDOW = 512
MOBA_BLOCK = 256
MOBA_TOPK = 3
D_FF = 4 * D_MODEL
PAGE_SIZE = 128
ROPE_THETA = 10000.0
NORM_EPS = 1e-6
NEG = -1e30
FORCED = 1e6
SCALE = HEAD_DIM ** -0.5

V7X_VMEM_BYTES = 64 * 1024 * 1024
VMEM_LIMIT = V7X_VMEM_BYTES - 8 * 1024 * 1024
KV_TILE = 512
NSA_Q_TILE = 128
MOBA_Q_TILE = MOBA_BLOCK


def _cparams(n_axes):
    return pltpu.CompilerParams(dimension_semantics=("arbitrary",) * n_axes,
                                vmem_limit_bytes=VMEM_LIMIT)


def _rms_rows(x, g):
    ms = jnp.mean(x * x, axis=-1, keepdims=True)
    return x * lax.rsqrt(ms + NORM_EPS) * g


def _sigmoid(x):
    return 1.0 / (1.0 + jnp.exp(-x))


_R_QA, _R_QB, _R_KB, _R_VB = 0, NSA_WIDTH, NSA_WIDTH + MOBA_WIDTH, NSA_WIDTH + 2 * MOBA_WIDTH
_R_KV4 = _R_VB + MOBA_WIDTH
_R_C = _R_KV4 + 4 * NSA_KV_WIDTH
_R_GA = _R_C + 2 * NSA_KV_WIDTH
_R_GM = _R_GA + HEAD_DIM
_R_END = _R_GM + 2 * D_MODEL
GM_CHUNK = 512


def _norm_rope_heads(acc, g_ref, cos, sin):
    nparts, fparts = [], []
    for hh in range(acc.shape[0] // HEAD_DIM):
        blk = acc[hh * HEAD_DIM:(hh + 1) * HEAD_DIM, :]
        ms = jnp.sum(blk * blk, axis=0, keepdims=True) * (1.0 / HEAD_DIM)
        y = blk * lax.rsqrt(ms + NORM_EPS) * g_ref[...]
        nparts.append(y)
        y1, y2 = y[:HALF, :], y[HALF:, :]
        fparts += [y1 * cos - y2 * sin, y2 * cos + y1 * sin]
    return jnp.concatenate(nparts, axis=0), jnp.concatenate(fparts, axis=0)


def _proj_kernel(x_ref, gn_ref, w_ref, gqa_ref, gqb_ref, gkb_ref, gks_ref, gkw_ref, cos_ref, sin_ref,
                 cmp_ref, sel_ref, win_ref, moba_ref, qn_ref, qr_ref, qb_ref, ga_ref, gm_ref, *copies):
    ht = _rms_rows(x_ref[0], gn_ref[...]).T.astype(BF16)
    cos, sin = cos_ref[...], sin_ref[...]
    kvw = NSA_KV_WIDTH
    mm = lambda r0, rows: jnp.dot(w_ref[r0:r0 + rows, :], ht, preferred_element_type=F32)
    c_rows_ref, k_rows_ref, v_tt_ref = copies if copies else (None, None, None)

    qn, qr = _norm_rope_heads(mm(_R_QA, NSA_WIDTH), gqa_ref, cos, sin)
    qn_ref[0] = qn.astype(BF16)
    qr_ref[0] = qr.astype(BF16)
    qb_ref[0] = _norm_rope_heads(mm(_R_QB, MOBA_WIDTH), gqb_ref, cos, sin)[1].astype(BF16)
    kb = _norm_rope_heads(mm(_R_KB, MOBA_WIDTH), gkb_ref, cos, sin)[1]
    vb = mm(_R_VB, MOBA_WIDTH)
    moba_ref[0, :MOBA_WIDTH, :] = kb
    moba_ref[0, MOBA_WIDTH:, :] = vb
    kv4 = mm(_R_KV4, 4 * kvw)
    ks = _norm_rope_heads(kv4[0:kvw], gks_ref, cos, sin)[1]
    kw = _norm_rope_heads(kv4[2 * kvw:3 * kvw], gkw_ref, cos, sin)[1]
    vs, vw = kv4[kvw:2 * kvw], kv4[3 * kvw:4 * kvw]
    sel_ref[0, :kvw, :] = ks
    sel_ref[0, kvw:, :] = vs
    win_ref[0, :kvw, :] = kw
    win_ref[0, kvw:, :] = vw
    c = mm(_R_C, 2 * kvw)
    cmp_ref[0] = c
    ga_ref[0] = _sigmoid(mm(_R_GA, HEAD_DIM))
    for k in range(2 * D_MODEL // GM_CHUNK):
        gm_ref[:, k * GM_CHUNK:(k + 1) * GM_CHUNK] = _sigmoid(mm(_R_GM + k * GM_CHUNK, GM_CHUNK)).T
    if copies:
        c_rows_ref[0] = c.T
        k_rows_ref[0, :, :MOBA_WIDTH] = kb.T.astype(BF16)
        k_rows_ref[0, :, MOBA_WIDTH:MOBA_WIDTH + kvw] = ks.T.astype(BF16)
        k_rows_ref[0, :, MOBA_WIDTH + kvw:] = kw.T.astype(BF16)
        v_tt_ref[0, 0, :MOBA_WIDTH, :] = vb.astype(BF16)
        v_tt_ref[0, 0, MOBA_WIDTH:MOBA_WIDTH + kvw, :] = vs.astype(BF16)
        v_tt_ref[0, 0, MOBA_WIDTH + kvw:, :] = vw.astype(BF16)


def _proj(x, gn, wt, gains, cos, sin, *, tm, attention_copies):
    b, t, d = x.shape
    kvw = NSA_KV_WIDTH
    n_pos_tiles = cos.shape[1] // tm
    tiles_per_kv = KV_TILE // tm if attention_copies else 1
    fixed2 = lambda bb, i: (0, 0)
    feat = lambda rows, dtype: (jax.ShapeDtypeStruct((b, rows, t), dtype),
                                pl.BlockSpec((1, rows, tm), lambda bb, i: (bb, 0, i)))
    outs = [feat(2 * kvw, F32), feat(2 * kvw, F32), feat(2 * kvw, F32), feat(2 * MOBA_WIDTH, F32),
            feat(NSA_WIDTH, BF16), feat(NSA_WIDTH, BF16), feat(MOBA_WIDTH, BF16), feat(HEAD_DIM, F32),
            (jax.ShapeDtypeStruct((b * t, 2 * d), F32),
             pl.BlockSpec((tm, 2 * d), lambda bb, i: (bb * (t // tm) + i, 0)))]
    if attention_copies:
        kcols = MOBA_WIDTH + 2 * kvw
        outs += [(jax.ShapeDtypeStruct((b, t, 2 * kvw), F32), pl.BlockSpec((1, tm, 2 * kvw), lambda bb, i: (bb, i, 0))),
                 (jax.ShapeDtypeStruct((b, t, kcols), BF16), pl.BlockSpec((1, tm, kcols), lambda bb, i: (bb, i, 0))),
                 (jax.ShapeDtypeStruct((b, t // KV_TILE, kcols, KV_TILE), BF16),
                  pl.BlockSpec((1, 1, kcols, tm), lambda bb, i: (bb, i // tiles_per_kv, 0, i % tiles_per_kv)))]
    res = pl.pallas_call(
        _proj_kernel,
        grid=(b, t // tm),
        in_specs=[pl.BlockSpec((1, tm, d), lambda bb, i: (bb, i, 0)),
                  pl.BlockSpec((1, d), fixed2),
                  pl.BlockSpec(wt.shape, fixed2)]
        + [pl.BlockSpec((HEAD_DIM, 1), fixed2)] * len(gains)
        + [pl.BlockSpec((HALF, tm), lambda bb, i: (0, i % n_pos_tiles))] * 2,
        out_specs=[o[1] for o in outs],
        out_shape=[o[0] for o in outs],
        compiler_params=_cparams(2),
        name="in_proj",
    )(x, gn.reshape(1, d), wt, *[g.reshape(HEAD_DIM, 1) for g in gains], cos, sin)
    names = ["cmp_t", "sel_t", "win_t", "moba_t", "qn_t", "qr_t", "qb_t", "ga_t", "gm", "c_rows", "k_rows", "v_tt"]
    return dict(zip(names, res))


def _gelu_tanh(x):
    return 0.5 * x * (1.0 + jnp.tanh(0.7978845608028654 * (x + 0.044715 * x * x * x)))


def _compress_blocks(load_rows, n_cmp, w1_ref, pos_ref, w2_ref, gk_ref):
    out = []
    for kv in range(2):
        sub = jnp.concatenate([load_rows(kv, j) for j in range(CMP_STRIDE)], axis=1).astype(BF16)
        hcat = jnp.dot(sub, w1_ref[kv], preferred_element_type=F32)
        bias = jnp.dot(pos_ref[kv], w1_ref[kv], preferred_element_type=F32)
        acc = None
        for g in range(NSA_KV_GROUPS):
            c0 = g * 2 * CMP_HIDDEN
            lo, hi = hcat[:, c0:c0 + CMP_HIDDEN], hcat[:, c0 + CMP_HIDDEN:c0 + 2 * CMP_HIDDEN]
            b = bias[0:1, c0:c0 + CMP_HIDDEN] + bias[1:2, c0 + CMP_HIDDEN:c0 + 2 * CMP_HIDDEN]
            hidden = lo + pltpu.roll(hi, n_cmp - 1, axis=0) + b
            part = jnp.dot(_gelu_tanh(hidden).astype(BF16), w2_ref[kv * NSA_KV_GROUPS + g],
                           preferred_element_type=F32)
            acc = part if acc is None else acc + part
        out.append(acc.T)
    parts = []
    for g in range(NSA_KV_GROUPS):
        blk = out[0][g * HEAD_DIM:(g + 1) * HEAD_DIM, :]
        ms = jnp.sum(blk * blk, axis=0, keepdims=True) * (1.0 / HEAD_DIM)
        parts.append(blk * lax.rsqrt(ms + NORM_EPS) * gk_ref[...])
    return jnp.concatenate(parts, axis=0), out[1]


def _compress_kernel(xk_ref, xv_ref, w1_ref, pos_ref, w2_ref, gk_ref, kc_ref, vct_ref):
    n_cmp = xk_ref.shape[1] // CMP_STRIDE
    x_refs = (xk_ref, xv_ref)
    kct, vct = _compress_blocks(lambda kv, j: x_refs[kv][0, pl.ds(j, n_cmp, stride=CMP_STRIDE), :],
                                n_cmp, w1_ref, pos_ref, w2_ref, gk_ref)
    kc_ref[0] = kct.T.astype(BF16)
    vct_ref[0] = vct.astype(BF16)


def _compress(c_rows, w1kv, poskv, w2pad, gk):
    nb, t, _ = c_rows.shape
    n_cmp = t // CMP_STRIDE
    full = lambda a: pl.BlockSpec(a.shape, lambda i: (0,) * a.ndim)
    return pl.pallas_call(
        _compress_kernel,
        grid=(nb,),
        in_specs=[pl.BlockSpec((1, t, NSA_KV_WIDTH), lambda i: (i, 0, 0)),
                  pl.BlockSpec((1, t, NSA_KV_WIDTH), lambda i: (i, 0, 1)),
                  full(w1kv), full(poskv), full(w2pad),
                  pl.BlockSpec((HEAD_DIM, 1), lambda i: (0, 0))],
        out_specs=[pl.BlockSpec((1, n_cmp, NSA_KV_WIDTH), lambda i: (i, 0, 0)),
                   pl.BlockSpec((1, NSA_KV_WIDTH, n_cmp), lambda i: (i, 0, 0))],
        out_shape=[jax.ShapeDtypeStruct((nb, n_cmp, NSA_KV_WIDTH), BF16),
                   jax.ShapeDtypeStruct((nb, NSA_KV_WIDTH, n_cmp), BF16)],
        compiler_params=_cparams(1),
        name="compress",
    )(c_rows, c_rows, w1kv, poskv, w2pad, gk.reshape(HEAD_DIM, 1))


M_FLOOR = -1e29


def _softmax_scores(s, m, acc, vt_aug):
    m_new = jnp.maximum(m, jnp.max(s, axis=0, keepdims=True))
    p = jnp.exp(s - m_new).astype(BF16)
    return m_new, jnp.exp(m - m_new) * acc + jnp.dot(vt_aug, p, preferred_element_type=F32)


def _rank_rows(score, blk):
    rank = jnp.zeros(score.shape, F32)
    for j in range(score.shape[0]):
        sj = score[j:j + 1, :]
        beats = (sj > score) | ((sj == score) & (blk > j))
        rank = rank + beats.astype(F32)
    return rank


def _rep_rows(row8, n_rows):
    return jnp.concatenate([row8] * (n_rows // 8), axis=0)


def _nsa_kernel(qn_ref, qr_ref, ga_ref, kc_ref, vct_ref, ks_ref, vst_ref, kw_ref, vwt_ref,
                o_ref, sel_ref, *, tq, tk, n_sel):
    qi = pl.program_id(1)
    q0 = qi * tq
    n = NSA_REP * tq
    n_cmp = kc_ref.shape[1]
    blocks_per_tile = tk // SEL_BLOCK
    lane = lax.broadcasted_iota(jnp.int32, (1, n), 1)
    t_lane = q0 + (lane & (tq - 1))
    t_q = q0 + lax.broadcasted_iota(jnp.int32, (1, tq), 1)
    cur = t_q >> 6
    blk = lax.broadcasted_iota(jnp.int32, (n_sel, 1), 0)
    cend = lax.broadcasted_iota(jnp.int32, (n_cmp, 1), 0) * CMP_STRIDE + (CMP_LEN - 1)
    krow = lax.broadcasted_iota(jnp.int32, (tk, 1), 0)
    ci = lax.broadcasted_iota(jnp.int32, (n_sel, n_cmp), 1)
    mi = lax.broadcasted_iota(jnp.int32, (n_sel, n_cmp), 0) * (SEL_BLOCK // CMP_STRIDE)
    amat = ((ci >= mi) & (ci <= mi + 3)).astype(F32) + ((ci >= mi - 1) & (ci <= mi + 2)).astype(F32)
    zeros_q = jnp.zeros((HEAD_DIM, n), BF16)
    ones_v = jnp.ones((HEAD_DIM, tk), BF16)

    def q_pad(ref, g):
        q = jnp.concatenate([ref[0, (g * NSA_REP + r) * HEAD_DIM:(g * NSA_REP + r + 1) * HEAD_DIM, :]
                             for r in range(NSA_REP)], axis=1) * SCALE
        return jnp.concatenate([q, zeros_q] if g == 0 else [zeros_q, q], axis=0)

    def gate_row(branch, g):
        return jnp.concatenate([ga_ref[0, branch * NSA_HEADS + g * NSA_REP + r:
                                       branch * NSA_HEADS + g * NSA_REP + r + 1, :]
                                for r in range(NSA_REP)], axis=1)

    groups = range(NSA_KV_GROUPS)
    g_rows = [slice(g * HEAD_DIM, (g + 1) * HEAD_DIM) for g in groups]
    qr = [q_pad(qr_ref, g) for g in groups]
    o_c = []
    for g in groups:
        rows = g_rows[g]
        qn = q_pad(qn_ref, g)
        sc = jnp.dot(kc_ref[0], qn, preferred_element_type=F32)
        mask_c = cend <= t_lane
        smc = jnp.where(mask_c, sc, NEG)
        e = jnp.where(mask_c, jnp.exp(smc - jnp.max(smc, axis=0, keepdims=True)), 0.0)
        den = jnp.sum(e, axis=0, keepdims=True)
        pc = e / jnp.where(den > 0.0, den, 1.0)
        o_c.append(jnp.dot(vct_ref[0], pc.astype(BF16), preferred_element_type=F32)[rows, :])
        imp_n = jnp.dot(amat, pc, preferred_element_type=F32, precision=lax.Precision.HIGHEST)
        imp = imp_n[:, 0:tq]
        for r in range(1, NSA_REP):
            imp = imp + imp_n[:, r * tq:(r + 1) * tq]
        forced = (blk == 0) | (blk >= cur - 1)
        score = jnp.where(blk > cur, NEG, jnp.where(forced, FORCED, imp))
        sel = (_rank_rows(score, blk) < SEL_TOPK) & (blk <= cur)
        sel_n = jnp.concatenate([jnp.where(sel, 0.0, NEG)] * NSA_REP, axis=1)
        for m_ in range(n_sel):
            sel_ref[g, m_] = jnp.broadcast_to(sel_n[m_:m_ + 1, :], (8, n))

    def tile(kv, carry, with_win, final):
        k0 = pl.multiple_of(kv * tk, tk)
        ks_t = ks_ref[0, pl.ds(k0, tk), :]
        raw = [jnp.dot(ks_t, qr[g], preferred_element_type=F32) for g in groups]
        if with_win:
            kw_t = kw_ref[0, pl.ds(k0, tk), :]
            raw += [jnp.dot(kw_t, qr[g], preferred_element_type=F32) for g in groups]
        out = []
        for g in groups:
            s = raw[g] + jnp.concatenate([_rep_rows(sel_ref[g, kv * blocks_per_tile + b_], SEL_BLOCK)
                                          for b_ in range(blocks_per_tile)], axis=0)
            if final:
                s = jnp.where((kv * tk + krow) <= t_lane, s, NEG)
            out.append(_softmax_scores(s, *carry[g], jnp.concatenate([vst_ref[0, kv, g_rows[g], :], ones_v], axis=0)))
        if not with_win:
            return tuple(out) + tuple(carry[NSA_KV_GROUPS:])
        rel = t_lane - (kv * tk + krow)
        keep = (rel >= 0) if final else (rel < WINDOW)
        for g in groups:
            s = jnp.where(keep, raw[NSA_KV_GROUPS + g], NEG)
            out.append(_softmax_scores(s, *carry[NSA_KV_GROUPS + g],
                                       jnp.concatenate([vwt_ref[0, kv, g_rows[g], :], ones_v], axis=0)))
        return tuple(out)

    init = tuple((jnp.full((1, n), M_FLOOR, F32), jnp.zeros((2 * HEAD_DIM, n), F32)) for _ in range(2 * NSA_KV_GROUPS))
    last = (q0 + tq - 1) // tk
    prev = jnp.maximum(last - 1, 0)
    carry = lax.fori_loop(0, prev, lambda kv, c: tile(kv, c, False, False), init)
    carry = lax.fori_loop(prev, last, lambda kv, c: tile(kv, c, True, False), carry)
    carry = tile(last, carry, True, True)

    for g in groups:
        acc_s, acc_w = carry[g][1], carry[NSA_KV_GROUPS + g][1]
        o_s = acc_s[:HEAD_DIM] / acc_s[HEAD_DIM:HEAD_DIM + 1]
        o_w = acc_w[:HEAD_DIM] / acc_w[HEAD_DIM:HEAD_DIM + 1]
        o = gate_row(0, g) * o_c[g] + gate_row(1, g) * o_s + gate_row(2, g) * o_w
        o_heads = jnp.concatenate([o[:, r * tq:(r + 1) * tq] for r in range(NSA_REP)], axis=0)
        o_ref[0, :, g * NSA_REP * HEAD_DIM:(g + 1) * NSA_REP * HEAD_DIM] = o_heads.T.astype(BF16)


def _nsa(qn_t, qr_t, ga_t, kc, vc_t, k_rows, ks_col, vs_tt, vs_row, kw_rows, kw_col, vw_tt, vw_row):
    b, _, tq_total = qn_t.shape
    tq, tk = NSA_Q_TILE, KV_TILE
    assert WINDOW == tk and tk % tq == 0 and tq_total % tk == 0
    n_cmp = kc.shape[1]
    l_sel = k_rows.shape[1]
    n_sel = ((l_sel // SEL_BLOCK) + 7) // 8 * 8
    l_win = kw_rows.shape[1]
    n = NSA_REP * tq
    return pl.pallas_call(
        functools.partial(_nsa_kernel, tq=tq, tk=tk, n_sel=n_sel),
        grid=(b, tq_total // tq),
        in_specs=[
            pl.BlockSpec((1, NSA_WIDTH, tq), lambda bb, i: (bb, 0, i)),
            pl.BlockSpec((1, NSA_WIDTH, tq), lambda bb, i: (bb, 0, i)),
            pl.BlockSpec((1, ga_t.shape[1], tq), lambda bb, i: (bb, 0, i)),
            pl.BlockSpec((1, n_cmp, NSA_KV_WIDTH), lambda bb, i: (bb, 0, 0)),
            pl.BlockSpec((1, NSA_KV_WIDTH, n_cmp), lambda bb, i: (bb, 0, 0)),
            pl.BlockSpec((1, l_sel, NSA_KV_WIDTH), lambda bb, i: (bb, 0, ks_col)),
            pl.BlockSpec((1, l_sel // tk, NSA_KV_WIDTH, tk), lambda bb, i: (bb, 0, vs_row, 0)),
            pl.BlockSpec((1, l_win, NSA_KV_WIDTH), lambda bb, i: (bb, 0, kw_col)),
            pl.BlockSpec((1, l_win // tk, NSA_KV_WIDTH, tk), lambda bb, i: (bb, 0, vw_row, 0)),
        ],
        out_specs=pl.BlockSpec((1, tq, NSA_WIDTH), lambda bb, i: (bb, i, 0)),
        out_shape=jax.ShapeDtypeStruct((b, tq_total, NSA_WIDTH), BF16),
        scratch_shapes=[pltpu.VMEM((NSA_KV_GROUPS, n_sel, 8, n), F32)],
        compiler_params=_cparams(2),
        name="nsa_attend",
    )(qn_t, qr_t, ga_t, kc, vc_t, k_rows, vs_tt, kw_rows, vw_tt)


MOBA_CHAINS = 4


def _moba_kernel(q_ref, k_ref, vt_ref, o_ref, km_ref, sel_ref, *, tq, tk, n_blk):
    qi = pl.program_id(2)
    q0 = qi * tq
    n = 2 * tq
    pair = 2 * HEAD_DIM
    cq = q0 // MOBA_BLOCK
    blocks_per_tile = tk // MOBA_BLOCK
    n_rows = km_ref.shape[1]

    @pl.when(qi == 0)
    def _():
        kall = k_ref[0].astype(F32)
        km = jnp.sum(kall.reshape(n_blk, MOBA_BLOCK, MOBA_CHAINS * pair), axis=1) * (1.0 / MOBA_BLOCK)
        if n_rows > n_blk:
            km = jnp.concatenate([km, jnp.zeros((n_rows - n_blk, MOBA_CHAINS * pair), F32)], axis=0)
        for c in range(MOBA_CHAINS):
            km_ref[c] = km[:, c * pair:(c + 1) * pair].astype(BF16)

    lane = lax.broadcasted_iota(jnp.int32, (1, n), 1)
    t_lane = q0 + (lane & (tq - 1))
    blk = lax.broadcasted_iota(jnp.int32, (n_rows, 1), 0)
    krow = lax.broadcasted_iota(jnp.int32, (MOBA_BLOCK, 1), 0)
    zq = jnp.zeros((HEAD_DIM, tq), BF16)
    ones_v = jnp.ones((16, tk), BF16)
    past = blk < cq
    qbds = []
    for c in range(MOBA_CHAINS):
        r0 = c * pair
        qbd = jnp.concatenate(
            [jnp.concatenate([q_ref[0, r0:r0 + HEAD_DIM, :] * SCALE, zq], axis=0),
             jnp.concatenate([zq, q_ref[0, r0 + HEAD_DIM:r0 + pair, :] * SCALE], axis=0)], axis=1)
        qbds.append(qbd)
        s_blk = jnp.dot(km_ref[c], qbd, preferred_element_type=F32) * (1.0 / SCALE)
        score = jnp.where(past, s_blk, NEG)
        sel_b = jnp.where((_rank_rows(score, blk) < MOBA_TOPK) & past, 0.0, NEG)
        for m_ in range(n_rows):
            sel_ref[c, m_] = jnp.broadcast_to(sel_b[m_:m_ + 1, :], (8, n))

    def tile(kv, carry, own_tile):
        raw = [jnp.dot(k_ref[0, pl.ds(pl.multiple_of(kv * tk, tk), tk), c * pair:(c + 1) * pair], qbds[c],
                       preferred_element_type=F32) for c in range(MOBA_CHAINS)]
        out = []
        for c in range(MOBA_CHAINS):
            m, acc = carry[c]
            parts = []
            for b_ in range(blocks_per_tile):
                nb = kv * blocks_per_tile + b_
                sb = raw[c][b_ * MOBA_BLOCK:(b_ + 1) * MOBA_BLOCK, :]
                picked = sb + _rep_rows(sel_ref[c, nb], MOBA_BLOCK)
                if own_tile:
                    own = jnp.where((nb * MOBA_BLOCK + krow) <= t_lane, sb, NEG)
                    picked = jnp.where(nb == cq, own, picked)
                parts.append(picked)
            vt_aug = jnp.concatenate([vt_ref[0, kv, c * pair:(c + 1) * pair, :], ones_v], axis=0)
            out.append(_softmax_scores(jnp.concatenate(parts, axis=0), m, acc, vt_aug))
        return tuple(out)

    init = tuple((jnp.full((1, n), M_FLOOR, F32), jnp.zeros((pair + 16, n), F32)) for _ in range(MOBA_CHAINS))
    last = (q0 + tq - 1) // tk
    carry = lax.fori_loop(0, last, lambda kv, cr: tile(kv, cr, False), init)
    carry = tile(last, carry, True)
    for c in range(MOBA_CHAINS):
        acc = carry[c][1]
        o = acc[:pair] / acc[pair:pair + 1]
        o2 = jnp.concatenate([o[0:HEAD_DIM, 0:tq], o[HEAD_DIM:pair, tq:2 * tq]], axis=0)
        o_ref[0, :, c * pair:(c + 1) * pair] = o2.T.astype(BF16)


def _moba(q_t, k_rows, k_col0, v_tt, v_row0):
    b, _, tq_total = q_t.shape
    tq, tk = MOBA_Q_TILE, KV_TILE
    width = MOBA_CHAINS * 2 * HEAD_DIM
    assert tq == MOBA_BLOCK and tk % tq == 0 and k_col0 % width == 0 and v_row0 % width == 0
    l_kv = k_rows.shape[1]
    n_blk = l_kv // MOBA_BLOCK
    n_rows = (n_blk + 7) // 8 * 8
    n = 2 * tq
    return pl.pallas_call(
        functools.partial(_moba_kernel, tq=tq, tk=tk, n_blk=n_blk),
        grid=(b, MOBA_WIDTH // width, tq_total // tq),
        in_specs=[
            pl.BlockSpec((1, width, tq), lambda bb, hp, i: (bb, hp, i)),
            pl.BlockSpec((1, l_kv, width), lambda bb, hp, i: (bb, 0, k_col0 // width + hp)),
            pl.BlockSpec((1, l_kv // tk, width, tk), lambda bb, hp, i: (bb, 0, v_row0 // width + hp, 0)),
        ],
        out_specs=pl.BlockSpec((1, tq, width), lambda bb, hp, i: (bb, i, hp)),
        out_shape=jax.ShapeDtypeStruct((b, tq_total, MOBA_WIDTH), BF16),
        scratch_shapes=[pltpu.VMEM((MOBA_CHAINS, n_rows, 2 * HEAD_DIM), BF16),
                        pltpu.VMEM((MOBA_CHAINS, n_rows, 8, n), F32)],
        compiler_params=_cparams(3),
        name="moba_attend",
    )(q_t, k_rows, v_tt)


def _merge_kernel(x_ref, oa_ref, ob_ref, gm_ref, wa_ref, wb_ref, wo_ref, o_ref):
    ua = jnp.dot(oa_ref[...], wa_ref[...], preferred_element_type=F32)
    ub = jnp.dot(ob_ref[...], wb_ref[...], preferred_element_type=F32)
    mixed = gm_ref[:, :D_MODEL] * ua + gm_ref[:, D_MODEL:] * ub
    o_ref[...] = x_ref[...] + jnp.dot(mixed.astype(BF16), wo_ref[...], preferred_element_type=F32)


def _merge(x2, oa, ob, gm, wa, wb, wo, *, tm):
    m, d = x2.shape
    row = lambda i: (i, 0)
    fixed = lambda i: (0, 0)
    return pl.pallas_call(
        _merge_kernel,
        grid=(m // tm,),
        in_specs=[pl.BlockSpec((tm, d), row), pl.BlockSpec((tm, NSA_WIDTH), row),
                  pl.BlockSpec((tm, MOBA_WIDTH), row), pl.BlockSpec((tm, 2 * d), row),
                  pl.BlockSpec(wa.shape, fixed), pl.BlockSpec(wb.shape, fixed), pl.BlockSpec(wo.shape, fixed)],
        out_specs=pl.BlockSpec((tm, d), row),
        out_shape=jax.ShapeDtypeStruct((m, d), F32),
        compiler_params=_cparams(1),
        name="merge_out",
    )(x2, oa, ob, gm, wa, wb, wo)


def _mlp_kernel(x_ref, g_ref, w1_ref, w2_ref, o_ref, h_ref, acc_ref):
    c = pl.program_id(1)

    @pl.when(c == 0)
    def _():
        h_ref[...] = _rms_rows(x_ref[...], g_ref[...]).astype(BF16)
        acc_ref[...] = x_ref[...]

    u = jnp.maximum(jnp.dot(h_ref[...], w1_ref[...], preferred_element_type=F32), 0.0)
    acc_ref[...] += jnp.dot((u * u).astype(BF16), w2_ref[...], preferred_element_type=F32)

    @pl.when(c == pl.num_programs(1) - 1)
    def _():
        o_ref[...] = acc_ref[...]


def _mlp(x2, g, w1, w2, *, tm, tf):
    m, d = x2.shape
    dff = w1.shape[1]
    return pl.pallas_call(
        _mlp_kernel,
        grid=(m // tm, dff // tf),
        in_specs=[pl.BlockSpec((tm, d), lambda i, c: (i, 0)),
                  pl.BlockSpec((1, d), lambda i, c: (0, 0)),
                  pl.BlockSpec((d, tf), lambda i, c: (0, c)),
                  pl.BlockSpec((tf, d), lambda i, c: (c, 0))],
        out_specs=pl.BlockSpec((tm, d), lambda i, c: (i, 0)),
        out_shape=jax.ShapeDtypeStruct((m, d), F32),
        scratch_shapes=[pltpu.VMEM((tm, d), BF16), pltpu.VMEM((tm, d), F32)],
        compiler_params=_cparams(2),
        name="sq_relu_mlp",
    )(x2, g.reshape(1, d), w1, w2)


LANES = 128


def _stack_rows(rows, n_rows=8):
    idx = lax.broadcasted_iota(jnp.int32, (n_rows, 1), 0)
    out = jnp.zeros((n_rows, rows[0].shape[1]), F32)
    for i, r in enumerate(rows):
        out = jnp.where(idx == i, r, out)
    return out


def _lane_col(x, j, lane):
    return jnp.sum(jnp.where(lane == j, x, 0.0), axis=1, keepdims=True)


def _col_tiles(row):
    m = jnp.broadcast_to(row, (LANES, LANES)).T
    return [m[:HEAD_DIM], m[HEAD_DIM:]]


def _row_of_cols(cols):
    m = jnp.broadcast_to(jnp.concatenate(cols, axis=0), (LANES, LANES)).T
    return m[0:1, :]


def _attend_pages(kts, vts, masks, q_tiles, k_new, v_new, new_ok, lane):
    outs = []
    for qh in q_tiles:
        s = [jnp.where(masks[p], jnp.sum(kts[p] * qh, axis=0, keepdims=True) * SCALE, NEG)
             for p in range(len(kts))]
        s_new = jnp.where(new_ok, jnp.sum(k_new * qh, axis=0, keepdims=True) * SCALE, NEG)
        m = s_new
        for sp in s:
            m = jnp.maximum(m, sp)
        m = jnp.max(m, axis=1, keepdims=True)
        e_new = jnp.where(new_ok & (lane == 0), jnp.exp(s_new - m), 0.0)
        l = e_new
        acc = e_new * v_new
        for p in range(len(kts)):
            e = jnp.where(masks[p], jnp.exp(s[p] - m), 0.0)
            l = l + e
            acc = acc + vts[p] * e
        outs.append(jnp.sum(acc, axis=1, keepdims=True) / jnp.sum(l, axis=1, keepdims=True))
    return outs


def _decode_kernel(pt_ref, *refs, n_pages, t_pos, w_buf):
    del pt_ref
    cmp_pages = refs[:n_pages]
    sel_pages = refs[n_pages:2 * n_pages]
    moba_pages = refs[2 * n_pages:3 * n_pages]
    (win_ref, q_ref, new_ref, ga_ref, w1_ref, pos_ref, w2_ref, gk_ref,
     oa_ref, ob_ref, xk_ref, xv_ref) = refs[3 * n_pages:]
    past = n_pages * PAGE_SIZE
    n_cmp = past // CMP_STRIDE
    lane = lax.broadcasted_iota(jnp.int32, (1, LANES), 1)
    sub_i = lax.broadcasted_iota(jnp.int32, (LANES, 1), 0)
    row8 = lax.broadcasted_iota(jnp.int32, (8, 1), 0)

    def tiles(ref, off, n_feat):
        out = []
        for c in range(n_feat // LANES):
            out += _col_tiles(ref[0, :, off + c * LANES:off + (c + 1) * LANES])
        return out

    qn, qr, qb = tiles(q_ref, 0, NSA_WIDTH), tiles(q_ref, NSA_WIDTH, NSA_WIDTH), tiles(q_ref, 2 * NSA_WIDTH, MOBA_WIDTH)
    kvw = NSA_KV_WIDTH
    ks_new, vs_new = tiles(new_ref, 0, kvw), tiles(new_ref, kvw, kvw)
    kw_new, vw_new = tiles(new_ref, 2 * kvw, kvw), tiles(new_ref, 3 * kvw, kvw)
    kb_new, vb_new = tiles(new_ref, 4 * kvw, MOBA_WIDTH), tiles(new_ref, 4 * kvw + MOBA_WIDTH, MOBA_WIDTH)
    ga_row = ga_ref[0]

    x_refs = (xk_ref, xv_ref)
    for p in range(n_pages):
        for kv in range(2):
            x_refs[kv][p * PAGE_SIZE:(p + 1) * PAGE_SIZE, :] = cmp_pages[p][kv].reshape(kvw, PAGE_SIZE).T
    ck_t = _compress_blocks(lambda kv, j: x_refs[kv][pl.ds(j, n_cmp, stride=CMP_STRIDE), :],
                            n_cmp, w1_ref, pos_ref, w2_ref, gk_ref)

    cend_ok = (lane * CMP_STRIDE + (CMP_LEN - 1)) <= t_pos
    cur = t_pos // SEL_BLOCK
    forced = (lane == 0) | (lane >= cur - 1)
    mi = lane * (SEL_BLOCK // CMP_STRIDE)
    amat_t = (((sub_i >= mi) & (sub_i <= mi + 3)).astype(F32)
              + ((sub_i >= mi - 1) & (sub_i <= mi + 2)).astype(F32))
    blocks_per_page = PAGE_SIZE // SEL_BLOCK

    def gate(branch, h):
        return _lane_col(ga_row, branch * NSA_HEADS + h, lane)

    nsa_cols = []
    for g in range(NSA_KV_GROUPS):
        rows = slice(g * HEAD_DIM, (g + 1) * HEAD_DIM)
        heads = range(g * NSA_REP, (g + 1) * NSA_REP)
        kct, vct = ck_t[0][rows, :], ck_t[1][rows, :]
        s4 = _stack_rows([jnp.sum(kct * qn[h], axis=0, keepdims=True) * SCALE for h in heads])
        mask4 = cend_ok & (row8 < NSA_REP)
        sm = jnp.where(mask4, s4, NEG)
        e = jnp.where(mask4, jnp.exp(sm - jnp.max(sm, axis=1, keepdims=True)), 0.0)
        den = jnp.sum(e, axis=1, keepdims=True)
        pc = e / jnp.where(den > 0.0, den, 1.0)
        o_c = [jnp.sum(vct * pc[r:r + 1, :], axis=1, keepdims=True) for r in range(NSA_REP)]
        psum = jnp.broadcast_to(jnp.sum(pc, axis=0, keepdims=True), (8, LANES))
        imp = jnp.dot(psum, amat_t, preferred_element_type=F32, precision=lax.Precision.HIGHEST)[0:1, :]
        score = jnp.where(lane > cur, NEG, jnp.where(forced, FORCED, imp))
        rm = jnp.broadcast_to(score, (LANES, LANES))
        cm = rm.T
        rank = jnp.sum(((cm > rm) | ((cm == rm) & (sub_i < lane))).astype(F32), axis=0, keepdims=True)
        sel = ((rank < SEL_TOPK) & (lane <= cur)).astype(F32)
        masks = []
        for p in range(n_pages):
            picks = [_lane_col(sel, p * blocks_per_page + b_, lane) for b_ in range(blocks_per_page)]
            mrow = picks[-1]
            for b_ in range(blocks_per_page - 2, -1, -1):
                mrow = jnp.where(lane < (b_ + 1) * SEL_BLOCK, picks[b_], mrow)
            masks.append(mrow > 0.5)
        new_ok = _lane_col(sel, cur, lane) > 0.5
        o_s = _attend_pages([sel_pages[p][0, g] for p in range(n_pages)],
                            [sel_pages[p][1, g] for p in range(n_pages)],
                            masks, [qr[h] for h in heads], ks_new[g], vs_new[g], new_ok, lane)
        wmasks = []
        for c in range(w_buf // LANES):
            rel = t_pos - (past - w_buf + c * LANES + lane)
            wmasks.append((rel >= 0) & (rel < WINDOW))
        o_w = _attend_pages([win_ref[0, g, :, c * LANES:(c + 1) * LANES] for c in range(w_buf // LANES)],
                            [win_ref[1, g, :, c * LANES:(c + 1) * LANES] for c in range(w_buf // LANES)],
                            wmasks, [qr[h] for h in heads], kw_new[g], vw_new[g], lane >= 0, lane)
        for r, h in enumerate(heads):
            nsa_cols.append(gate(0, h) * o_c[r] + gate(1, h) * o_s[r] + gate(2, h) * o_w[r])
    for c in range(NSA_HEADS // 2):
        oa_ref[0, :, c * LANES:(c + 1) * LANES] = _row_of_cols(nsa_cols[2 * c:2 * c + 2])

    pages_per_blk = MOBA_BLOCK // PAGE_SIZE
    cq = t_pos // MOBA_BLOCK
    raw = [_stack_rows([jnp.sum(moba_pages[p][0, h] * qb[h], axis=0, keepdims=True) for h in range(MOBA_HEADS)])
           for p in range(n_pages)]
    s_blk = jnp.zeros((MOBA_HEADS, LANES), F32)
    for n_ in range(past // MOBA_BLOCK):
        tot = raw[n_ * pages_per_blk]
        for k_ in range(1, pages_per_blk):
            tot = tot + raw[n_ * pages_per_blk + k_]
        s_blk = jnp.where(lane == n_, jnp.sum(tot, axis=1, keepdims=True) * (1.0 / MOBA_BLOCK), s_blk)
    past_m = lane < cq
    score = jnp.where(past_m, s_blk, NEG)
    rank = jnp.zeros((MOBA_HEADS, LANES), F32)
    for j in range(cq):
        cj = _lane_col(score, j, lane)
        rank = rank + ((cj > score) | ((cj == score) & (lane > j))).astype(F32)
    sel = (past_m & (rank < MOBA_TOPK)).astype(F32)
    picked = [_lane_col(sel, n_, lane) > 0.5 for n_ in range(past // MOBA_BLOCK)]
    s_new = _stack_rows([jnp.sum(kb_new[h] * qb[h], axis=0, keepdims=True) for h in range(MOBA_HEADS)]) * SCALE
    sc = [jnp.where(picked[p // pages_per_blk], raw[p] * SCALE, NEG) for p in range(n_pages)]
    m = s_new
    for sp in sc:
        m = jnp.maximum(m, sp)
    m = jnp.max(m, axis=1, keepdims=True)
    e_new = jnp.where(lane == 0, jnp.exp(s_new - m), 0.0)
    es = [jnp.where(picked[p // pages_per_blk], jnp.exp(sc[p] - m), 0.0) for p in range(n_pages)]
    l = e_new
    for e in es:
        l = l + e
    den = jnp.sum(l, axis=1, keepdims=True)
    moba_cols = []
    for h in range(MOBA_HEADS):
        acc = e_new[h:h + 1, :] * vb_new[h]
        for p in range(n_pages):
            acc = acc + moba_pages[p][1, h] * es[p][h:h + 1, :]
        moba_cols.append(jnp.sum(acc, axis=1, keepdims=True) / den[h:h + 1, :])
    for c in range(MOBA_HEADS // 2):
        ob_ref[0, :, c * LANES:(c + 1) * LANES] = _row_of_cols(moba_cols[2 * c:2 * c + 2])


def _decode(cmp_t, sel_t, moba_t, win_t, layer, page_table, q_rows, new_rows, ga_rows, w1kv, poskv, w2pad, gk):
    n_seq, n_pages = page_table.shape
    past = n_pages * PAGE_SIZE
    w_buf = win_t.shape[-1]
    assert past // CMP_STRIDE == LANES and past % MOBA_BLOCK == 0 and w_buf % LANES == 0

    def page_spec(h, p):
        return pl.BlockSpec((None, None, 2, h, HEAD_DIM, PAGE_SIZE), lambda s, pt: (layer, pt[s, p], 0, 0, 0, 0))

    row_spec = lambda a: pl.BlockSpec((1, 1, a.shape[2]), lambda s, pt: (s, 0, 0))
    full = lambda a: pl.BlockSpec(a.shape, lambda s, pt: (0,) * a.ndim)
    in_specs = ([page_spec(NSA_KV_GROUPS, p) for p in range(n_pages)]
                + [page_spec(NSA_KV_GROUPS, p) for p in range(n_pages)]
                + [page_spec(MOBA_HEADS, p) for p in range(n_pages)]
                + [pl.BlockSpec((None, None, 2, NSA_KV_GROUPS, HEAD_DIM, w_buf), lambda s, pt: (layer, s, 0, 0, 0, 0)),
                   row_spec(q_rows), row_spec(new_rows), row_spec(ga_rows),
                   full(w1kv), full(poskv), full(w2pad), pl.BlockSpec((HEAD_DIM, 1), lambda s, pt: (0, 0))])
    out_spec = pl.BlockSpec((1, 1, NSA_WIDTH), lambda s, pt: (s, 0, 0))
    return pl.pallas_call(
        functools.partial(_decode_kernel, n_pages=n_pages, t_pos=past, w_buf=w_buf),
        grid_spec=pltpu.PrefetchScalarGridSpec(
            num_scalar_prefetch=1, grid=(n_seq,), in_specs=in_specs, out_specs=[out_spec, out_spec],
            scratch_shapes=[pltpu.VMEM((past, NSA_KV_WIDTH), F32)] * 2),
        out_shape=[jax.ShapeDtypeStruct((n_seq, 1, NSA_WIDTH), F32)] * 2,
        compiler_params=_cparams(1),
        name="decode_mixers",
    )(page_table, *([cmp_t] * n_pages), *([sel_t] * n_pages), *([moba_t] * n_pages), win_t,
      q_rows, new_rows, ga_rows, w1kv, poskv, w2pad, gk.reshape(HEAD_DIM, 1))


_SPLITS = (NSA_WIDTH, NSA_KV_WIDTH, NSA_KV_WIDTH, NSA_KV_WIDTH, NSA_KV_WIDTH, NSA_KV_WIDTH, NSA_KV_WIDTH,
           3 * NSA_HEADS, MOBA_WIDTH, MOBA_WIDTH, MOBA_WIDTH, 2 * D_MODEL)
_NAMES = ("qa", "kc", "vc", "ks", "vs", "kw", "vw", "ga", "qb", "kb", "vb", "gm")


def _layer_params(l, w_in, gq_nsa, gk_sel, gk_win, gq_moba, gk_moba, cmp_pos_k, cmp_pos_v,
                  cmp_w1_k, cmp_w2_k, cmp_w1_v, cmp_w2_v):
    wt = w_in[l].T
    off, part = 0, {}
    for name, size in zip(_NAMES, _SPLITS):
        part[name] = wt[off:off + size]
        off += size
    ga_pad = jnp.zeros((HEAD_DIM - 3 * NSA_HEADS, D_MODEL), F32)
    order = ("qa", "qb", "kb", "vb", "ks", "vs", "kw", "vw", "kc", "vc", "ga")
    p = {
        "wt": jnp.concatenate([part[n_] for n_ in order] + [ga_pad, part["gm"]], axis=0).astype(BF16),
        "gains": (gq_nsa[l], gq_moba[l], gk_moba[l], gk_sel[l], gk_win[l]),
    }
    w1kv, poskv, w2pad = [], [], []
    for w1, w2, pos in ((cmp_w1_k[l], cmp_w2_k[l], cmp_pos_k[l]), (cmp_w1_v[l], cmp_w2_v[l], cmp_pos_v[l])):
        w1r = w1.reshape(2, CMP_STRIDE, HEAD_DIM, CMP_HIDDEN)
        lohi = jnp.concatenate([w1r[0], w1r[1]], axis=-1)
        z = jnp.zeros_like(lohi)
        w1kv.append(jnp.stack([jnp.concatenate([lohi, z], axis=-1), jnp.concatenate([z, lohi], axis=-1)], axis=1)
                    .reshape(CMP_STRIDE * NSA_KV_WIDTH, NSA_KV_GROUPS * 2 * CMP_HIDDEN))
        posr = jnp.tile(pos.reshape(2, CMP_STRIDE, 1, HEAD_DIM), (1, 1, NSA_KV_GROUPS, 1))
        poskv.append(jnp.pad(posr.reshape(2, CMP_STRIDE * NSA_KV_WIDTH), ((0, 6), (0, 0))))
        zw = jnp.zeros_like(w2)
        w2pad += [jnp.concatenate([w2, zw], axis=1), jnp.concatenate([zw, w2], axis=1)]
    p["w1kv"] = jnp.stack(w1kv).astype(BF16)
    p["poskv"] = jnp.stack(poskv).astype(BF16)
    p["w2pad"] = jnp.stack(w2pad).astype(BF16)
    return p


def _rope_tables(pos):
    inv = ROPE_THETA ** (-jnp.arange(HALF, dtype=F32) / HALF)
    ang = inv[:, None] * pos.astype(F32)[None, :]
    return jnp.cos(ang), jnp.sin(ang)


def _cache_leaf(feat_major, heads):
    b, _, t = feat_major.shape
    return feat_major.reshape(b, 2, heads, HEAD_DIM, t).transpose(0, 4, 1, 2, 3)


def kernel(x_prompt, x_sample, cache_nsa_cmp, cache_nsa_sel, cache_moba, state_nsa_win, page_table,
           norm_mix, w_in, gq_nsa, gk_cmp, gk_sel, gk_win, gq_moba, gk_moba,
           cmp_pos_k, cmp_pos_v, cmp_w1_k, cmp_w2_k, cmp_w1_v, cmp_w2_v,
           w_up_nsa, w_up_moba, w_out, norm_mlp, w_mlp_up, w_mlp_down):
    depth = w_in.shape[0]
    b, t, d = x_prompt.shape
    n_dec = x_sample.shape[0]
    past_len = page_table.shape[1] * PAGE_SIZE

    cos_p, sin_p = _rope_tables(jnp.arange(t, dtype=jnp.int32))
    cos_s, sin_s = _rope_tables(jnp.full((n_dec,), past_len, dtype=jnp.int32))
    to_t = lambda c: c.transpose(0, 1, 3, 4, 5, 2)
    cmp_t, sel_t, moba_t, win_t = to_t(cache_nsa_cmp), to_t(cache_nsa_sel), to_t(cache_moba), to_t(state_nsa_win)

    xp = x_prompt
    xs = x_sample.reshape(1, n_dec, d)
    leaves = [[] for _ in range(8)]
    tm_p, tm_proj = 512, 256
    kvw = NSA_KV_WIDTH
    for l in range(depth):
        p = _layer_params(l, w_in, gq_nsa, gk_sel, gk_win, gq_moba, gk_moba, cmp_pos_k, cmp_pos_v,
                          cmp_w1_k, cmp_w2_k, cmp_w1_v, cmp_w2_v)
        wa, wb, wo = w_up_nsa[l].astype(BF16), w_up_moba[l].astype(BF16), w_out[l].astype(BF16)
        w1, w2 = w_mlp_up[l].astype(BF16), w_mlp_down[l].astype(BF16)

        pr = _proj(xp, norm_mix[l], p["wt"], p["gains"], cos_p, sin_p, tm=tm_proj, attention_copies=True)
        kc, vc_t = _compress(pr["c_rows"], p["w1kv"], p["poskv"], p["w2pad"], gk_cmp[l])
        ks_blk, kw_blk = MOBA_WIDTH // kvw, MOBA_WIDTH // kvw + 1
        oa = _nsa(pr["qn_t"], pr["qr_t"], pr["ga_t"], kc, vc_t,
                  pr["k_rows"], ks_blk, pr["v_tt"], ks_blk, pr["k_rows"], kw_blk, pr["v_tt"], kw_blk)
        ob = _moba(pr["qb_t"], pr["k_rows"], 0, pr["v_tt"], 0)
        x2 = _merge(xp.reshape(b * t, d), oa.reshape(b * t, NSA_WIDTH), ob.reshape(b * t, MOBA_WIDTH),
                    pr["gm"], wa, wb, wo, tm=tm_p)
        xp = _mlp(x2, norm_mlp[l], w1, w2, tm=tm_p, tf=1024).reshape(b, t, d)
        leaves[0].append(_cache_leaf(pr["cmp_t"], NSA_KV_GROUPS))
        leaves[1].append(_cache_leaf(pr["sel_t"], NSA_KV_GROUPS))
        leaves[2].append(_cache_leaf(pr["moba_t"], MOBA_HEADS))
        leaves[3].append(_cache_leaf(pr["win_t"][:, :, t - min(WINDOW, t):], NSA_KV_GROUPS))

        sr = _proj(xs, norm_mix[l], p["wt"], p["gains"], cos_s, sin_s, tm=n_dec, attention_copies=False)
        seq_rows = lambda a: a[0].T.astype(F32)
        c_new, new_sel, new_win, new_moba = (seq_rows(sr[k_]) for k_ in ("cmp_t", "sel_t", "win_t", "moba_t"))
        q_rows = jnp.concatenate([seq_rows(sr["qn_t"]), seq_rows(sr["qr_t"]), seq_rows(sr["qb_t"])], axis=1)
        new_rows = jnp.concatenate([new_sel, new_win, new_moba], axis=1)
        ga_rows = jnp.pad(seq_rows(sr["ga_t"]), ((0, 0), (0, LANES - HEAD_DIM)))
        oa, ob = _decode(cmp_t, sel_t, moba_t, win_t, l, page_table, q_rows[:, None, :], new_rows[:, None, :],
                         ga_rows[:, None, :], p["w1kv"], p["poskv"], p["w2pad"], gk_cmp[l])
        x2 = _merge(xs.reshape(n_dec, d), oa[:, 0].astype(BF16), ob[:, 0].astype(BF16), sr["gm"], wa, wb, wo, tm=n_dec)
        xs = _mlp(x2, norm_mlp[l], w1, w2, tm=n_dec, tf=1024).reshape(1, n_dec, d)
        seq_leaf = lambda rows, heads: rows.reshape(n_dec, 1, 2, heads, HEAD_DIM)
        leaves[4].append(seq_leaf(c_new, NSA_KV_GROUPS))
        leaves[5].append(seq_leaf(new_sel, NSA_KV_GROUPS))
        leaves[6].append(seq_leaf(new_moba, MOBA_HEADS))
        win_all = jnp.concatenate([state_nsa_win[l], seq_leaf(new_win, NSA_KV_GROUPS)], axis=1)
        leaves[7].append(win_all[:, -min(WINDOW, past_len + 1):])
    return (xp, xs.reshape(n_dec, 1, d)) + tuple(jnp.stack(v) for v in leaves)
```

```python
import functools

import jax
import jax.numpy as jnp
from jax import lax
from jax.experimental import pallas as pl
from jax.experimental.pallas import tpu as pltpu

F32 = jnp.float32
BF16 = jnp.bfloat16

D_MODEL = 1024
HEAD_DIM = 64
HALF = HEAD_DIM // 2
NSA_HEADS = 8
NSA_KV_GROUPS = 2
NSA_REP = NSA_HEADS // NSA_KV_GROUPS
MOBA_HEADS = 8
NSA_WIDTH = NSA_HEADS * HEAD_DIM
NSA_KV_WIDTH = NSA_KV_GROUPS * HEAD_DIM
MOBA_WIDTH = MOBA_HEADS * HEAD_DIM
CMP_LEN = 32
CMP_STRIDE = 16
CMP_HIDDEN = 2 * HEAD_DIM
SEL_BLOCK = 64
SEL_TOPK = 16
WINDOW = 512
MOBA_BLOCK = 256
MOBA_TOPK = 3
D_FF = 4 * D_MODEL
PAGE_SIZE = 128
ROPE_THETA = 10000.0
NORM_EPS = 1e-6
NEG = -1e30
FORCED = 1e6
SCALE = HEAD_DIM ** -0.5

V7X_VMEM_BYTES = 64 * 1024 * 1024
VMEM_LIMIT = V7X_VMEM_BYTES - 8 * 1024 * 1024
KV_TILE = 512
NSA_Q_TILE = 128
MOBA_Q_TILE = MOBA_BLOCK


def _cparams(n_axes):
    return pltpu.CompilerParams(dimension_semantics=("arbitrary",) * n_axes,
                                vmem_limit_bytes=VMEM_LIMIT)


def _rms_rows(x, g):
    ms = jnp.mean(x * x, axis=-1, keepdims=True)
    return x * lax.rsqrt(ms + NORM_EPS) * g


def _sigmoid(x):
    return 1.0 / (1.0 + jnp.exp(-x))


_R_QA, _R_QB, _R_KB, _R_VB = 0, NSA_WIDTH, NSA_WIDTH + MOBA_WIDTH, NSA_WIDTH + 2 * MOBA_WIDTH
_R_KV4 = _R_VB + MOBA_WIDTH
_R_C = _R_KV4 + 4 * NSA_KV_WIDTH
_R_GA = _R_C + 2 * NSA_KV_WIDTH
_R_GM = _R_GA + HEAD_DIM
_R_END = _R_GM + 2 * D_MODEL
GM_CHUNK = 512


def _norm_rope_heads(acc, g_ref, cos, sin):
    nparts, fparts = [], []
    for hh in range(acc.shape[0] // HEAD_DIM):
        blk = acc[hh * HEAD_DIM:(hh + 1) * HEAD_DIM, :]
        ms = jnp.sum(blk * blk, axis=0, keepdims=True) * (1.0 / HEAD_DIM)
        y = blk * lax.rsqrt(ms + NORM_EPS) * g_ref[...]
        nparts.append(y)
        y1, y2 = y[:HALF, :], y[HALF:, :]
        fparts += [y1 * cos - y2 * sin, y2 * cos + y1 * sin]
    return jnp.concatenate(nparts, axis=0), jnp.concatenate(fparts, axis=0)


def _proj_kernel(x_ref, gn_ref, w_ref, gqa_ref, gqb_ref, gkb_ref, gks_ref, gkw_ref, cos_ref, sin_ref,
                 cmp_ref, sel_ref, win_ref, moba_ref, qn_ref, qr_ref, qb_ref, ga_ref, gm_ref, *copies):
    ht = _rms_rows(x_ref[0], gn_ref[...]).T.astype(BF16)
    cos, sin = cos_ref[...], sin_ref[...]
    kvw = NSA_KV_WIDTH
    mm = lambda r0, rows: jnp.dot(w_ref[r0:r0 + rows, :], ht, preferred_element_type=F32)
    c_rows_ref, k_rows_ref, v_tt_ref = copies if copies else (None, None, None)

    qn, qr = _norm_rope_heads(mm(_R_QA, NSA_WIDTH), gqa_ref, cos, sin)
    qn_ref[0] = qn.astype(BF16)
    qr_ref[0] = qr.astype(BF16)
    qb_ref[0] = _norm_rope_heads(mm(_R_QB, MOBA_WIDTH), gqb_ref, cos, sin)[1].astype(BF16)
    kb = _norm_rope_heads(mm(_R_KB, MOBA_WIDTH), gkb_ref, cos, sin)[1]
    vb = mm(_R_VB, MOBA_WIDTH)
    moba_ref[0, :MOBA_WIDTH, :] = kb
    moba_ref[0, MOBA_WIDTH:, :] = vb
    kv4 = mm(_R_KV4, 4 * kvw)
    ks = _norm_rope_heads(kv4[0:kvw], gks_ref, cos, sin)[1]
    kw = _norm_rope_heads(kv4[2 * kvw:3 * kvw], gkw_ref, cos, sin)[1]
    vs, vw = kv4[kvw:2 * kvw], kv4[3 * kvw:4 * kvw]
    sel_ref[0, :kvw, :] = ks
    sel_ref[0, kvw:, :] = vs
    win_ref[0, :kvw, :] = kw
    win_ref[0, kvw:, :] = vw
    c = mm(_R_C, 2 * kvw)
    cmp_ref[0] = c
    ga_ref[0] = _sigmoid(mm(_R_GA, HEAD_DIM))
    for k in range(2 * D_MODEL // GM_CHUNK):
        gm_ref[:, k * GM_CHUNK:(k + 1) * GM_CHUNK] = _sigmoid(mm(_R_GM + k * GM_CHUNK, GM_CHUNK)).T
    if copies:
        c_rows_ref[0] = c.T
        k_rows_ref[0, :, :MOBA_WIDTH] = kb.T.astype(BF16)
        k_rows_ref[0, :, MOBA_WIDTH:MOBA_WIDTH + kvw] = ks.T.astype(BF16)
        k_rows_ref[0, :, MOBA_WIDTH + kvw:] = kw.T.astype(BF16)
        v_tt_ref[0, 0, :MOBA_WIDTH, :] = vb.astype(BF16)
        v_tt_ref[0, 0, MOBA_WIDTH:MOBA_WIDTH + kvw, :] = vs.astype(BF16)
        v_tt_ref[0, 0, MOBA_WIDTH + kvw:, :] = vw.astype(BF16)


def _proj(x, gn, wt, gains, cos, sin, *, tm, attention_copies):
    b, t, d = x.shape
    kvw = NSA_KV_WIDTH
    n_pos_tiles = cos.shape[1] // tm
    tiles_per_kv = KV_TILE // tm if attention_copies else 1
    fixed2 = lambda bb, i: (0, 0)
    feat = lambda rows, dtype: (jax.ShapeDtypeStruct((b, rows, t), dtype),
                                pl.BlockSpec((1, rows, tm), lambda bb, i: (bb, 0, i)))
    outs = [feat(2 * kvw, F32), feat(2 * kvw, F32), feat(2 * kvw, F32), feat(2 * MOBA_WIDTH, F32),
            feat(NSA_WIDTH, BF16), feat(NSA_WIDTH, BF16), feat(MOBA_WIDTH, BF16), feat(HEAD_DIM, F32),
            (jax.ShapeDtypeStruct((b * t, 2 * d), F32),
             pl.BlockSpec((tm, 2 * d), lambda bb, i: (bb * (t // tm) + i, 0)))]
    if attention_copies:
        kcols = MOBA_WIDTH + 2 * kvw
        outs += [(jax.ShapeDtypeStruct((b, t, 2 * kvw), F32), pl.BlockSpec((1, tm, 2 * kvw), lambda bb, i: (bb, i, 0))),
                 (jax.ShapeDtypeStruct((b, t, kcols), BF16), pl.BlockSpec((1, tm, kcols), lambda bb, i: (bb, i, 0))),
                 (jax.ShapeDtypeStruct((b, t // KV_TILE, kcols, KV_TILE), BF16),
                  pl.BlockSpec((1, 1, kcols, tm), lambda bb, i: (bb, i // tiles_per_kv, 0, i % tiles_per_kv)))]
    res = pl.pallas_call(
        _proj_kernel,
        grid=(b, t // tm),
        in_specs=[pl.BlockSpec((1, tm, d), lambda bb, i: (bb, i, 0)),
                  pl.BlockSpec((1, d), fixed2),
                  pl.BlockSpec(wt.shape, fixed2)]
        + [pl.BlockSpec((HEAD_DIM, 1), fixed2)] * len(gains)
        + [pl.BlockSpec((HALF, tm), lambda bb, i: (0, i % n_pos_tiles))] * 2,
        out_specs=[o[1] for o in outs],
        out_shape=[o[0] for o in outs],
        compiler_params=_cparams(2),
        name="in_proj",
    )(x, gn.reshape(1, d), wt, *[g.reshape(HEAD_DIM, 1) for g in gains], cos, sin)
    names = ["cmp_t", "sel_t", "win_t", "moba_t", "qn_t", "qr_t", "qb_t", "ga_t", "gm", "c_rows", "k_rows", "v_tt"]
    return dict(zip(names, res))


def _gelu_tanh(x):
    return 0.5 * x * (1.0 + jnp.tanh(0.7978845608028654 * (x + 0.044715 * x * x * x)))


def _compress_blocks(load_rows, n_cmp, w1_ref, pos_ref, w2_ref, gk_ref):
    out = []
    for kv in range(2):
        sub = jnp.concatenate([load_rows(kv, j) for j in range(CMP_STRIDE)], axis=1).astype(BF16)
        hcat = jnp.dot(sub, w1_ref[kv], preferred_element_type=F32)
        bias = jnp.dot(pos_ref[kv], w1_ref[kv], preferred_element_type=F32)
        acc = None
        for g in range(NSA_KV_GROUPS):
            c0 = g * 2 * CMP_HIDDEN
            lo, hi = hcat[:, c0:c0 + CMP_HIDDEN], hcat[:, c0 + CMP_HIDDEN:c0 + 2 * CMP_HIDDEN]
            b = bias[0:1, c0:c0 + CMP_HIDDEN] + bias[1:2, c0 + CMP_HIDDEN:c0 + 2 * CMP_HIDDEN]
            hidden = lo + pltpu.roll(hi, n_cmp - 1, axis=0) + b
            part = jnp.dot(_gelu_tanh(hidden).astype(BF16), w2_ref[kv * NSA_KV_GROUPS + g],
                           preferred_element_type=F32)
            acc = part if acc is None else acc + part
        out.append(acc.T)
    parts = []
    for g in range(NSA_KV_GROUPS):
        blk = out[0][g * HEAD_DIM:(g + 1) * HEAD_DIM, :]
        ms = jnp.sum(blk * blk, axis=0, keepdims=True) * (1.0 / HEAD_DIM)
        parts.append(blk * lax.rsqrt(ms + NORM_EPS) * gk_ref[...])
    return jnp.concatenate(parts, axis=0), out[1]


def _compress_kernel(xk_ref, xv_ref, w1_ref, pos_ref, w2_ref, gk_ref, kc_ref, vct_ref):
    n_cmp = xk_ref.shape[1] // CMP_STRIDE
    x_refs = (xk_ref, xv_ref)
    kct, vct = _compress_blocks(lambda kv, j: x_refs[kv][0, pl.ds(j, n_cmp, stride=CMP_STRIDE), :],
                                n_cmp, w1_ref, pos_ref, w2_ref, gk_ref)
    kc_ref[0] = kct.T.astype(BF16)
    vct_ref[0] = vct.astype(BF16)


def _compress(c_rows, w1kv, poskv, w2pad, gk):
    nb, t, _ = c_rows.shape
    n_cmp = t // CMP_STRIDE
    full = lambda a: pl.BlockSpec(a.shape, lambda i: (0,) * a.ndim)
    return pl.pallas_call(
        _compress_kernel,
        grid=(nb,),
        in_specs=[pl.BlockSpec((1, t, NSA_KV_WIDTH), lambda i: (i, 0, 0)),
                  pl.BlockSpec((1, t, NSA_KV_WIDTH), lambda i: (i, 0, 1)),
                  full(w1kv), full(poskv), full(w2pad),
                  pl.BlockSpec((HEAD_DIM, 1), lambda i: (0, 0))],
        out_specs=[pl.BlockSpec((1, n_cmp, NSA_KV_WIDTH), lambda i: (i, 0, 0)),
                   pl.BlockSpec((1, NSA_KV_WIDTH, n_cmp), lambda i: (i, 0, 0))],
        out_shape=[jax.ShapeDtypeStruct((nb, n_cmp, NSA_KV_WIDTH), BF16),
                   jax.ShapeDtypeStruct((nb, NSA_KV_WIDTH, n_cmp), BF16)],
        compiler_params=_cparams(1),
        name="compress",
    )(c_rows, c_rows, w1kv, poskv, w2pad, gk.reshape(HEAD_DIM, 1))


M_FLOOR = -1e29


def _softmax_scores(s, m, acc, vt_aug):
    m_new = jnp.maximum(m, jnp.max(s, axis=0, keepdims=True))
    p = jnp.exp(s - m_new).astype(BF16)
    return m_new, jnp.exp(m - m_new) * acc + jnp.dot(vt_aug, p, preferred_element_type=F32)


def _rank_rows(score, blk):
    rank = jnp.zeros(score.shape, F32)
    for j in range(score.shape[0]):
        sj = score[j:j + 1, :]
        beats = (sj > score) | ((sj == score) & (blk > j))
        rank = rank + beats.astype(F32)
    return rank


def _rep_rows(row8, n_rows):
    return jnp.concatenate([row8] * (n_rows // 8), axis=0)


def _nsa_kernel(qn_ref, qr_ref, ga_ref, kc_ref, vct_ref, ks_ref, vst_ref, kw_ref, vwt_ref,
                o_ref, sel_ref, *, tq, tk, n_sel):
    qi = pl.program_id(1)
    q0 = qi * tq
    n = NSA_REP * tq
    n_cmp = kc_ref.shape[1]
    blocks_per_tile = tk // SEL_BLOCK
    lane = lax.broadcasted_iota(jnp.int32, (1, n), 1)
    t_lane = q0 + (lane & (tq - 1))
    t_q = q0 + lax.broadcasted_iota(jnp.int32, (1, tq), 1)
    cur = t_q >> 6
    blk = lax.broadcasted_iota(jnp.int32, (n_sel, 1), 0)
    cend = lax.broadcasted_iota(jnp.int32, (n_cmp, 1), 0) * CMP_STRIDE + (CMP_LEN - 1)
    krow = lax.broadcasted_iota(jnp.int32, (tk, 1), 0)
    ci = lax.broadcasted_iota(jnp.int32, (n_sel, n_cmp), 1)
    mi = lax.broadcasted_iota(jnp.int32, (n_sel, n_cmp), 0) * (SEL_BLOCK // CMP_STRIDE)
    amat = ((ci >= mi) & (ci <= mi + 3)).astype(F32) + ((ci >= mi - 1) & (ci <= mi + 2)).astype(F32)
    zeros_q = jnp.zeros((HEAD_DIM, n), BF16)
    ones_v = jnp.ones((HEAD_DIM, tk), BF16)

    def q_pad(ref, g):
        q = jnp.concatenate([ref[0, (g * NSA_REP + r) * HEAD_DIM:(g * NSA_REP + r + 1) * HEAD_DIM, :]
                             for r in range(NSA_REP)], axis=1) * SCALE
        return jnp.concatenate([q, zeros_q] if g == 0 else [zeros_q, q], axis=0)

    def gate_row(branch, g):
        return jnp.concatenate([ga_ref[0, branch * NSA_HEADS + g * NSA_REP + r:
                                       branch * NSA_HEADS + g * NSA_REP + r + 1, :]
                                for r in range(NSA_REP)], axis=1)

    groups = range(NSA_KV_GROUPS)
    g_rows = [slice(g * HEAD_DIM, (g + 1) * HEAD_DIM) for g in groups]
    qr = [q_pad(qr_ref, g) for g in groups]
    o_c = []
    for g in groups:
        rows = g_rows[g]
        qn = q_pad(qn_ref, g)
        sc = jnp.dot(kc_ref[0], qn, preferred_element_type=F32)
        mask_c = cend <= t_lane
        smc = jnp.where(mask_c, sc, NEG)
        e = jnp.where(mask_c, jnp.exp(smc - jnp.max(smc, axis=0, keepdims=True)), 0.0)
        den = jnp.sum(e, axis=0, keepdims=True)
        pc = e / jnp.where(den > 0.0, den, 1.0)
        o_c.append(jnp.dot(vct_ref[0], pc.astype(BF16), preferred_element_type=F32)[rows, :])
        imp_n = jnp.dot(amat, pc, preferred_element_type=F32, precision=lax.Precision.HIGHEST)
        imp = imp_n[:, 0:tq]
        for r in range(1, NSA_REP):
            imp = imp + imp_n[:, r * tq:(r + 1) * tq]
        forced = (blk == 0) | (blk >= cur - 1)
        score = jnp.where(blk > cur, NEG, jnp.where(forced, FORCED, imp))
        sel = (_rank_rows(score, blk) < SEL_TOPK) & (blk <= cur)
        sel_n = jnp.concatenate([jnp.where(sel, 0.0, NEG)] * NSA_REP, axis=1)
        for m_ in range(n_sel):
            sel_ref[g, m_] = jnp.broadcast_to(sel_n[m_:m_ + 1, :], (8, n))

    def tile(kv, carry, with_win, final):
        k0 = pl.multiple_of(kv * tk, tk)
        ks_t = ks_ref[0, pl.ds(k0, tk), :]
        raw = [jnp.dot(ks_t, qr[g], preferred_element_type=F32) for g in groups]
        if with_win:
            kw_t = kw_ref[0, pl.ds(k0, tk), :]
            raw += [jnp.dot(kw_t, qr[g], preferred_element_type=F32) for g in groups]
        out = []
        for g in groups:
            s = raw[g] + jnp.concatenate([_rep_rows(sel_ref[g, kv * blocks_per_tile + b_], SEL_BLOCK)
                                          for b_ in range(blocks_per_tile)], axis=0)
            if final:
                s = jnp.where((kv * tk + krow) <= t_lane, s, NEG)
            out.append(_softmax_scores(s, *carry[g], jnp.concatenate([vst_ref[0, kv, g_rows[g], :], ones_v], axis=0)))
        if not with_win:
            return tuple(out) + tuple(carry[NSA_KV_GROUPS:])
        rel = t_lane - (kv * tk + krow)
        keep = (rel >= 0) if final else (rel < WINDOW)
        for g in groups:
            s = jnp.where(keep, raw[NSA_KV_GROUPS + g], NEG)
            out.append(_softmax_scores(s, *carry[NSA_KV_GROUPS + g],
                                       jnp.concatenate([vwt_ref[0, kv, g_rows[g], :], ones_v], axis=0)))
        return tuple(out)

    init = tuple((jnp.full((1, n), M_FLOOR, F32), jnp.zeros((2 * HEAD_DIM, n), F32)) for _ in range(2 * NSA_KV_GROUPS))
    last = (q0 + tq - 1) // tk
    prev = jnp.maximum(last - 1, 0)
    carry = lax.fori_loop(0, prev, lambda kv, c: tile(kv, c, False, False), init)
    carry = lax.fori_loop(prev, last, lambda kv, c: tile(kv, c, True, False), carry)
    carry = tile(last, carry, True, True)

    for g in groups:
        acc_s, acc_w = carry[g][1], carry[NSA_KV_GROUPS + g][1]
        o_s = acc_s[:HEAD_DIM] / acc_s[HEAD_DIM:HEAD_DIM + 1]
        o_w = acc_w[:HEAD_DIM] / acc_w[HEAD_DIM:HEAD_DIM + 1]
        o = gate_row(0, g) * o_c[g] + gate_row(1, g) * o_s + gate_row(2, g) * o_w
        o_heads = jnp.concatenate([o[:, r * tq:(r + 1) * tq] for r in range(NSA_REP)], axis=0)
        o_ref[0, :, g * NSA_REP * HEAD_DIM:(g + 1) * NSA_REP * HEAD_DIM] = o_heads.T.astype(BF16)


def _nsa(qn_t, qr_t, ga_t, kc, vc_t, k_rows, ks_col, vs_tt, vs_row, kw_rows, kw_col, vw_tt, vw_row):
    b, _, tq_total = qn_t.shape
    tq, tk = NSA_Q_TILE, KV_TILE
    assert WINDOW == tk and tk % tq == 0 and tq_total % tk == 0
    n_cmp = kc.shape[1]
    l_sel = k_rows.shape[1]
    n_sel = ((l_sel // SEL_BLOCK) + 7) // 8 * 8
    l_win = kw_rows.shape[1]
    n = NSA_REP * tq
    return pl.pallas_call(
        functools.partial(_nsa_kernel, tq=tq, tk=tk, n_sel=n_sel),
        grid=(b, tq_total // tq),
        in_specs=[
            pl.BlockSpec((1, NSA_WIDTH, tq), lambda bb, i: (bb, 0, i)),
            pl.BlockSpec((1, NSA_WIDTH, tq), lambda bb, i: (bb, 0, i)),
            pl.BlockSpec((1, ga_t.shape[1], tq), lambda bb, i: (bb, 0, i)),
            pl.BlockSpec((1, n_cmp, NSA_KV_WIDTH), lambda bb, i: (bb, 0, 0)),
            pl.BlockSpec((1, NSA_KV_WIDTH, n_cmp), lambda bb, i: (bb, 0, 0)),
            pl.BlockSpec((1, l_sel, NSA_KV_WIDTH), lambda bb, i: (bb, 0, ks_col)),
            pl.BlockSpec((1, l_sel // tk, NSA_KV_WIDTH, tk), lambda bb, i: (bb, 0, vs_row, 0)),
            pl.BlockSpec((1, l_win, NSA_KV_WIDTH), lambda bb, i: (bb, 0, kw_col)),
            pl.BlockSpec((1, l_win // tk, NSA_KV_WIDTH, tk), lambda bb, i: (bb, 0, vw_row, 0)),
        ],
        out_specs=pl.BlockSpec((1, tq, NSA_WIDTH), lambda bb, i: (bb, i, 0)),
        out_shape=jax.ShapeDtypeStruct((b, tq_total, NSA_WIDTH), BF16),
        scratch_shapes=[pltpu.VMEM((NSA_KV_GROUPS, n_sel, 8, n), F32)],
        compiler_params=_cparams(2),
        name="nsa_attend",
    )(qn_t, qr_t, ga_t, kc, vc_t, k_rows, vs_tt, kw_rows, vw_tt)


MOBA_CHAINS = 4


def _moba_kernel(q_ref, k_ref, vt_ref, o_ref, km_ref, sel_ref, *, tq, tk, n_blk):
    qi = pl.program_id(2)
    q0 = qi * tq
    n = 2 * tq
    pair = 2 * HEAD_DIM
    cq = q0 // MOBA_BLOCK
    blocks_per_tile = tk // MOBA_BLOCK
    n_rows = km_ref.shape[1]

    @pl.when(qi == 0)
    def _():
        kall = k_ref[0].astype(F32)
        km = jnp.sum(kall.reshape(n_blk, MOBA_BLOCK, MOBA_CHAINS * pair), axis=1) * (1.0 / MOBA_BLOCK)
        if n_rows > n_blk:
            km = jnp.concatenate([km, jnp.zeros((n_rows - n_blk, MOBA_CHAINS * pair), F32)], axis=0)
        for c in range(MOBA_CHAINS):
            km_ref[c] = km[:, c * pair:(c + 1) * pair].astype(BF16)

    lane = lax.broadcasted_iota(jnp.int32, (1, n), 1)
    t_lane = q0 + (lane & (tq - 1))
    blk = lax.broadcasted_iota(jnp.int32, (n_rows, 1), 0)
    krow = lax.broadcasted_iota(jnp.int32, (MOBA_BLOCK, 1), 0)
    zq = jnp.zeros((HEAD_DIM, tq), BF16)
    ones_v = jnp.ones((16, tk), BF16)
    past = blk < cq
    qbds = []
    for c in range(MOBA_CHAINS):
        r0 = c * pair
        qbd = jnp.concatenate(
            [jnp.concatenate([q_ref[0, r0:r0 + HEAD_DIM, :] * SCALE, zq], axis=0),
             jnp.concatenate([zq, q_ref[0, r0 + HEAD_DIM:r0 + pair, :] * SCALE], axis=0)], axis=1)
        qbds.append(qbd)
        s_blk = jnp.dot(km_ref[c], qbd, preferred_element_type=F32) * (1.0 / SCALE)
        score = jnp.where(past, s_blk, NEG)
        sel_b = jnp.where((_rank_rows(score, blk) < MOBA_TOPK) & past, 0.0, NEG)
        for m_ in range(n_rows):
            sel_ref[c, m_] = jnp.broadcast_to(sel_b[m_:m_ + 1, :], (8, n))

    def tile(kv, carry, own_tile):
        raw = [jnp.dot(k_ref[0, pl.ds(pl.multiple_of(kv * tk, tk), tk), c * pair:(c + 1) * pair], qbds[c],
                       preferred_element_type=F32) for c in range(MOBA_CHAINS)]
        out = []
        for c in range(MOBA_CHAINS):
            m, acc = carry[c]
            parts = []
            for b_ in range(blocks_per_tile):
                nb = kv * blocks_per_tile + b_
                sb = raw[c][b_ * MOBA_BLOCK:(b_ + 1) * MOBA_BLOCK, :]
                picked = sb + _rep_rows(sel_ref[c, nb], MOBA_BLOCK)
                if own_tile:
                    own = jnp.where((nb * MOBA_BLOCK + krow) <= t_lane, sb, NEG)
                    picked = jnp.where(nb == cq, own, picked)
                parts.append(picked)
            vt_aug = jnp.concatenate([vt_ref[0, kv, c * pair:(c + 1) * pair, :], ones_v], axis=0)
            out.append(_softmax_scores(jnp.concatenate(parts, axis=0), m, acc, vt_aug))
        return tuple(out)

    init = tuple((jnp.full((1, n), M_FLOOR, F32), jnp.zeros((pair + 16, n), F32)) for _ in range(MOBA_CHAINS))
    last = (q0 + tq - 1) // tk
    carry = lax.fori_loop(0, last, lambda kv, cr: tile(kv, cr, False), init)
    carry = tile(last, carry, True)
    for c in range(MOBA_CHAINS):
        acc = carry[c][1]
        o = acc[:pair] / acc[pair:pair + 1]
        o2 = jnp.concatenate([o[0:HEAD_DIM, 0:tq], o[HEAD_DIM:pair, tq:2 * tq]], axis=0)
        o_ref[0, :, c * pair:(c + 1) * pair] = o2.T.astype(BF16)


def _moba(q_t, k_rows, k_col0, v_tt, v_row0):
    b, _, tq_total = q_t.shape
    tq, tk = MOBA_Q_TILE, KV_TILE
    width = MOBA_CHAINS * 2 * HEAD_DIM
    assert tq == MOBA_BLOCK and tk % tq == 0 and k_col0 % width == 0 and v_row0 % width == 0
    l_kv = k_rows.shape[1]
    n_blk = l_kv // MOBA_BLOCK
    n_rows = (n_blk + 7) // 8 * 8
    n = 2 * tq
    return pl.pallas_call(
        functools.partial(_moba_kernel, tq=tq, tk=tk, n_blk=n_blk),
        grid=(b, MOBA_WIDTH // width, tq_total // tq),
        in_specs=[
            pl.BlockSpec((1, width, tq), lambda bb, hp, i: (bb, hp, i)),
            pl.BlockSpec((1, l_kv, width), lambda bb, hp, i: (bb, 0, k_col0 // width + hp)),
            pl.BlockSpec((1, l_kv // tk, width, tk), lambda bb, hp, i: (bb, 0, v_row0 // width + hp, 0)),
        ],
        out_specs=pl.BlockSpec((1, tq, width), lambda bb, hp, i: (bb, i, hp)),
        out_shape=jax.ShapeDtypeStruct((b, tq_total, MOBA_WIDTH), BF16),
        scratch_shapes=[pltpu.VMEM((MOBA_CHAINS, n_rows, 2 * HEAD_DIM), BF16),
                        pltpu.VMEM((MOBA_CHAINS, n_rows, 8, n), F32)],
        compiler_params=_cparams(3),
        name="moba_attend",
    )(q_t, k_rows, v_tt)


def _merge_kernel(x_ref, oa_ref, ob_ref, gm_ref, wa_ref, wb_ref, wo_ref, o_ref):
    ua = jnp.dot(oa_ref[...], wa_ref[...], preferred_element_type=F32)
    ub = jnp.dot(ob_ref[...], wb_ref[...], preferred_element_type=F32)
    mixed = gm_ref[:, :D_MODEL] * ua + gm_ref[:, D_MODEL:] * ub
    o_ref[...] = x_ref[...] + jnp.dot(mixed.astype(BF16), wo_ref[...], preferred_element_type=F32)


def _merge(x2, oa, ob, gm, wa, wb, wo, *, tm):
    m, d = x2.shape
    row = lambda i: (i, 0)
    fixed = lambda i: (0, 0)
    return pl.pallas_call(
        _merge_kernel,
        grid=(m // tm,),
        in_specs=[pl.BlockSpec((tm, d), row), pl.BlockSpec((tm, NSA_WIDTH), row),
                  pl.BlockSpec((tm, MOBA_WIDTH), row), pl.BlockSpec((tm, 2 * d), row),
                  pl.BlockSpec(wa.shape, fixed), pl.BlockSpec(wb.shape, fixed), pl.BlockSpec(wo.shape, fixed)],
        out_specs=pl.BlockSpec((tm, d), row),
        out_shape=jax.ShapeDtypeStruct((m, d), F32),
        compiler_params=_cparams(1),
        name="merge_out",
    )(x2, oa, ob, gm, wa, wb, wo)


def _mlp_kernel(x_ref, g_ref, w1_ref, w2_ref, o_ref, h_ref, acc_ref):
    c = pl.program_id(1)

    @pl.when(c == 0)
    def _():
        h_ref[...] = _rms_rows(x_ref[...], g_ref[...]).astype(BF16)
        acc_ref[...] = x_ref[...]

    u = jnp.maximum(jnp.dot(h_ref[...], w1_ref[...], preferred_element_type=F32), 0.0)
    acc_ref[...] += jnp.dot((u * u).astype(BF16), w2_ref[...], preferred_element_type=F32)

    @pl.when(c == pl.num_programs(1) - 1)
    def _():
        o_ref[...] = acc_ref[...]


def _mlp(x2, g, w1, w2, *, tm, tf):
    m, d = x2.shape
    dff = w1.shape[1]
    return pl.pallas_call(
        _mlp_kernel,
        grid=(m // tm, dff // tf),
        in_specs=[pl.BlockSpec((tm, d), lambda i, c: (i, 0)),
                  pl.BlockSpec((1, d), lambda i, c: (0, 0)),
                  pl.BlockSpec((d, tf), lambda i, c: (0, c)),
                  pl.BlockSpec((tf, d), lambda i, c: (c, 0))],
        out_specs=pl.BlockSpec((tm, d), lambda i, c: (i, 0)),
        out_shape=jax.ShapeDtypeStruct((m, d), F32),
        scratch_shapes=[pltpu.VMEM((tm, d), BF16), pltpu.VMEM((tm, d), F32)],
        compiler_params=_cparams(2),
        name="sq_relu_mlp",
    )(x2, g.reshape(1, d), w1, w2)


LANES = 128


def _stack_rows(rows, n_rows=8):
    idx = lax.broadcasted_iota(jnp.int32, (n_rows, 1), 0)
    out = jnp.zeros((n_rows, rows[0].shape[1]), F32)
    for i, r in enumerate(rows):
        out = jnp.where(idx == i, r, out)
    return out


def _lane_col(x, j, lane):
    return jnp.sum(jnp.where(lane == j, x, 0.0), axis=1, keepdims=True)


def _heads_on_rows(q_row, n_heads, feats):
    rows8 = lax.broadcasted_iota(jnp.int32, (8, feats), 0)
    lanes = lax.broadcasted_iota(jnp.int32, (8, feats), 1)
    if feats == n_heads * HEAD_DIM:
        return jnp.where((lanes // HEAD_DIM) == rows8, jnp.broadcast_to(q_row, (8, feats)), 0.0)
    rows = []
    for h in range(n_heads):
        c, off, g = h // 2, (h % 2) * HEAD_DIM, h // NSA_REP
        r = q_row[:, c * LANES:(c + 1) * LANES]
        rows.append(r if off == g * HEAD_DIM else pltpu.roll(r, HEAD_DIM, axis=1))
    q8 = _stack_rows(rows)
    return jnp.where((lanes // HEAD_DIM) == (rows8 // NSA_REP), q8, 0.0)


def _attend_pages(q8, kts, vts, masks, k_new, v_new, new_ok):
    qb = q8.astype(BF16)
    s = [jnp.where(masks[p], jnp.dot(qb, kts[p].astype(BF16), preferred_element_type=F32), NEG)
         for p in range(len(kts))]
    s_new = jnp.where(new_ok, jnp.sum(q8 * k_new, axis=1, keepdims=True), NEG)
    m = s[0]
    for sp in s[1:]:
        m = jnp.maximum(m, sp)
    m = jnp.maximum(jnp.max(m, axis=1, keepdims=True), s_new)
    e_new = jnp.where(new_ok, jnp.exp(s_new - m), 0.0)
    den = e_new
    acc = e_new * v_new
    for p in range(len(kts)):
        e = jnp.where(masks[p], jnp.exp(s[p] - m), 0.0)
        den = den + jnp.sum(e, axis=1, keepdims=True)
        acc = acc + lax.dot_general(e.astype(BF16), vts[p].astype(BF16), (((1,), (1,)), ((), ())),
                                    preferred_element_type=F32)
    return acc / den


def _decode_kernel(pt_ref, *refs, n_pages, t_pos, w_buf):
    del pt_ref
    cmp_pages = refs[:n_pages]
    sel_pages = refs[n_pages:2 * n_pages]
    moba_pages = refs[2 * n_pages:3 * n_pages]
    (win_ref, q_ref, new_ref, ga_ref, w1_ref, pos_ref, w2_ref, gk_ref,
     oa_ref, ob_ref, xk_ref, xv_ref) = refs[3 * n_pages:]
    past = n_pages * PAGE_SIZE
    n_cmp = past // CMP_STRIDE
    lane = lax.broadcasted_iota(jnp.int32, (1, LANES), 1)
    sub_i = lax.broadcasted_iota(jnp.int32, (LANES, 1), 0)
    row8 = lax.broadcasted_iota(jnp.int32, (8, 1), 0)

    kvw = NSA_KV_WIDTH
    qn8 = _heads_on_rows(q_ref[0, :, 0:NSA_WIDTH], NSA_HEADS, kvw) * SCALE
    qr8 = _heads_on_rows(q_ref[0, :, NSA_WIDTH:2 * NSA_WIDTH], NSA_HEADS, kvw) * SCALE
    qb8 = _heads_on_rows(q_ref[0, :, 2 * NSA_WIDTH:2 * NSA_WIDTH + MOBA_WIDTH], MOBA_HEADS, MOBA_WIDTH)
    new = lambda off, width: new_ref[0, :, off:off + width]
    ks_new, vs_new, kw_new, vw_new = new(0, kvw), new(kvw, kvw), new(2 * kvw, kvw), new(3 * kvw, kvw)
    kb_new, vb_new = new(4 * kvw, MOBA_WIDTH), new(4 * kvw + MOBA_WIDTH, MOBA_WIDTH)
    ga_cols = jnp.broadcast_to(ga_ref[0], (LANES, LANES)).T
    gate = [ga_cols[br * NSA_HEADS:(br + 1) * NSA_HEADS, 0:1] for br in range(3)]
    grp8 = row8 // NSA_REP

    x_refs = (xk_ref, xv_ref)
    for p in range(n_pages):
        for kv in range(2):
            x_refs[kv][p * PAGE_SIZE:(p + 1) * PAGE_SIZE, :] = cmp_pages[p][kv].reshape(kvw, PAGE_SIZE).T
    ck_t = _compress_blocks(lambda kv, j: x_refs[kv][pl.ds(j, n_cmp, stride=CMP_STRIDE), :],
                            n_cmp, w1_ref, pos_ref, w2_ref, gk_ref)

    cend_ok = (lane * CMP_STRIDE + (CMP_LEN - 1)) <= t_pos
    cur = t_pos // SEL_BLOCK
    forced = (lane == 0) | (lane >= cur - 1)
    mi = lane * (SEL_BLOCK // CMP_STRIDE)
    amat_t = (((sub_i >= mi) & (sub_i <= mi + 3)).astype(F32)
              + ((sub_i >= mi - 1) & (sub_i <= mi + 2)).astype(F32))
    blocks_per_page = PAGE_SIZE // SEL_BLOCK

    kct, vct = ck_t
    sm = jnp.where(cend_ok, jnp.dot(qn8.astype(BF16), kct.astype(BF16), preferred_element_type=F32), NEG)
    e = jnp.where(cend_ok, jnp.exp(sm - jnp.max(sm, axis=1, keepdims=True)), 0.0)
    den = jnp.sum(e, axis=1, keepdims=True)
    pc = e / jnp.where(den > 0.0, den, 1.0)
    o_c = lax.dot_general(pc.astype(BF16), vct.astype(BF16), (((1,), (1,)), ((), ())), preferred_element_type=F32)
    sel_rows = []
    for g in range(NSA_KV_GROUPS):
        psum = jnp.broadcast_to(jnp.sum(jnp.where(grp8 == g, pc, 0.0), axis=0, keepdims=True), (8, LANES))
        imp = jnp.dot(psum, amat_t, preferred_element_type=F32, precision=lax.Precision.HIGHEST)[0:1, :]
        score = jnp.where(lane > cur, NEG, jnp.where(forced, FORCED, imp))
        rm = jnp.broadcast_to(score, (LANES, LANES))
        cm = rm.T
        rank = jnp.sum(((cm > rm) | ((cm == rm) & (sub_i < lane))).astype(F32), axis=0, keepdims=True)
        sel_rows.append(((rank < SEL_TOPK) & (lane <= cur)).astype(F32))
    sel8 = jnp.where(grp8 == 0, sel_rows[0], sel_rows[1])
    masks = []
    for p in range(n_pages):
        picks = [_lane_col(sel8, p * blocks_per_page + b_, lane) for b_ in range(blocks_per_page)]
        mrow = picks[-1]
        for b_ in range(blocks_per_page - 2, -1, -1):
            mrow = jnp.where(lane < (b_ + 1) * SEL_BLOCK, picks[b_], mrow)
        masks.append(mrow > 0.5)
    page2d = lambda ref, kv: ref[kv].reshape(kvw, ref.shape[-1])
    o_s = _attend_pages(qr8, [page2d(sel_pages[p], 0) for p in range(n_pages)],
                        [page2d(sel_pages[p], 1) for p in range(n_pages)], masks,
                        ks_new, vs_new, _lane_col(sel8, cur, lane) > 0.5)
    wmasks = []
    for c in range(w_buf // LANES):
        rel = t_pos - (past - w_buf + c * LANES + lane)
        wmasks.append((rel >= 0) & (rel < WINDOW))
    win_k, win_v = page2d(win_ref, 0), page2d(win_ref, 1)
    o_w = _attend_pages(qr8, [win_k[:, c * LANES:(c + 1) * LANES] for c in range(w_buf // LANES)],
                        [win_v[:, c * LANES:(c + 1) * LANES] for c in range(w_buf // LANES)], wmasks,
                        kw_new, vw_new, row8 >= 0)
    o8 = gate[0] * o_c + gate[1] * o_s + gate[2] * o_w
    for c in range(NSA_HEADS // 2):
        even, odd = o8[2 * c:2 * c + 1, :], o8[2 * c + 1:2 * c + 2, :]
        if (2 * c) // NSA_REP == 0:
            odd = pltpu.roll(odd, HEAD_DIM, axis=1)
        else:
            even = pltpu.roll(even, HEAD_DIM, axis=1)
        oa_ref[0, :, c * LANES:(c + 1) * LANES] = jnp.where(lane < HEAD_DIM, even, odd)

    pages_per_blk = MOBA_BLOCK // PAGE_SIZE
    cq = t_pos // MOBA_BLOCK
    mpage = lambda p, kv: moba_pages[p][kv].reshape(MOBA_WIDTH, PAGE_SIZE).astype(BF16)
    qbb = qb8.astype(BF16)
    raw = [jnp.dot(qbb, mpage(p, 0), preferred_element_type=F32) for p in range(n_pages)]
    s_blk = jnp.zeros((MOBA_HEADS, LANES), F32)
    for n_ in range(past // MOBA_BLOCK):
        tot = raw[n_ * pages_per_blk]
        for k_ in range(1, pages_per_blk):
            tot = tot + raw[n_ * pages_per_blk + k_]
        s_blk = jnp.where(lane == n_, jnp.sum(tot, axis=1, keepdims=True) * (1.0 / MOBA_BLOCK), s_blk)
    past_m = lane < cq
    score = jnp.where(past_m, s_blk, NEG)
    rank = jnp.zeros((MOBA_HEADS, LANES), F32)
    for j in range(cq):
        cj = _lane_col(score, j, lane)
        rank = rank + ((cj > score) | ((cj == score) & (lane > j))).astype(F32)
    sel = (past_m & (rank < MOBA_TOPK)).astype(F32)
    picked = [_lane_col(sel, n_, lane) > 0.5 for n_ in range(past // MOBA_BLOCK)]
    s_new = jnp.sum(qb8 * kb_new, axis=1, keepdims=True) * SCALE
    sc = [jnp.where(picked[p // pages_per_blk], raw[p] * SCALE, NEG) for p in range(n_pages)]
    m = sc[0]
    for sp in sc[1:]:
        m = jnp.maximum(m, sp)
    m = jnp.maximum(jnp.max(m, axis=1, keepdims=True), s_new)
    e_new = jnp.exp(s_new - m)
    den = e_new
    acc = e_new * vb_new
    for p in range(n_pages):
        e = jnp.where(picked[p // pages_per_blk], jnp.exp(sc[p] - m), 0.0)
        den = den + jnp.sum(e, axis=1, keepdims=True)
        acc = acc + lax.dot_general(e.astype(BF16), mpage(p, 1), (((1,), (1,)), ((), ())),
                                    preferred_element_type=F32)
    rows8w = lax.broadcasted_iota(jnp.int32, (MOBA_HEADS, MOBA_WIDTH), 0)
    lanes_w = lax.broadcasted_iota(jnp.int32, (MOBA_HEADS, MOBA_WIDTH), 1)
    ob_ref[0] = jnp.sum(jnp.where((lanes_w // HEAD_DIM) == rows8w, acc / den, 0.0), axis=0, keepdims=True)


def _decode(cmp_t, sel_t, moba_t, win_t, layer, page_table, q_rows, new_rows, ga_rows, w1kv, poskv, w2pad, gk):
    n_seq, n_pages = page_table.shape
    past = n_pages * PAGE_SIZE
    w_buf = win_t.shape[-1]
    assert past // CMP_STRIDE == LANES and past % MOBA_BLOCK == 0 and w_buf % LANES == 0

    def page_spec(h, p):
        return pl.BlockSpec((None, None, 2, h, HEAD_DIM, PAGE_SIZE), lambda s, pt: (layer, pt[s, p], 0, 0, 0, 0))

    row_spec = lambda a: pl.BlockSpec((1, 1, a.shape[2]), lambda s, pt: (s, 0, 0))
    full = lambda a: pl.BlockSpec(a.shape, lambda s, pt: (0,) * a.ndim)
    in_specs = ([page_spec(NSA_KV_GROUPS, p) for p in range(n_pages)]
                + [page_spec(NSA_KV_GROUPS, p) for p in range(n_pages)]
                + [page_spec(MOBA_HEADS, p) for p in range(n_pages)]
                + [pl.BlockSpec((None, None, 2, NSA_KV_GROUPS, HEAD_DIM, w_buf), lambda s, pt: (layer, s, 0, 0, 0, 0)),
                   row_spec(q_rows), row_spec(new_rows), row_spec(ga_rows),
                   full(w1kv), full(poskv), full(w2pad), pl.BlockSpec((HEAD_DIM, 1), lambda s, pt: (0, 0))])
    out_spec = pl.BlockSpec((1, 1, NSA_WIDTH), lambda s, pt: (s, 0, 0))
    return pl.pallas_call(
        functools.partial(_decode_kernel, n_pages=n_pages, t_pos=past, w_buf=w_buf),
        grid_spec=pltpu.PrefetchScalarGridSpec(
            num_scalar_prefetch=1, grid=(n_seq,), in_specs=in_specs, out_specs=[out_spec, out_spec],
            scratch_shapes=[pltpu.VMEM((past, NSA_KV_WIDTH), F32)] * 2),
        out_shape=[jax.ShapeDtypeStruct((n_seq, 1, NSA_WIDTH), F32)] * 2,
        compiler_params=_cparams(1),
        name="decode_mixers",
    )(page_table, *([cmp_t] * n_pages), *([sel_t] * n_pages), *([moba_t] * n_pages), win_t,
      q_rows, new_rows, ga_rows, w1kv, poskv, w2pad, gk.reshape(HEAD_DIM, 1))


_SPLITS = (NSA_WIDTH, NSA_KV_WIDTH, NSA_KV_WIDTH, NSA_KV_WIDTH, NSA_KV_WIDTH, NSA_KV_WIDTH, NSA_KV_WIDTH,
           3 * NSA_HEADS, MOBA_WIDTH, MOBA_WIDTH, MOBA_WIDTH, 2 * D_MODEL)
_NAMES = ("qa", "kc", "vc", "ks", "vs", "kw", "vw", "ga", "qb", "kb", "vb", "gm")


def _layer_params(l, w_in, gq_nsa, gk_sel, gk_win, gq_moba, gk_moba, cmp_pos_k, cmp_pos_v,
                  cmp_w1_k, cmp_w2_k, cmp_w1_v, cmp_w2_v):
    wt = w_in[l].T
    off, part = 0, {}
    for name, size in zip(_NAMES, _SPLITS):
        part[name] = wt[off:off + size]
        off += size
    ga_pad = jnp.zeros((HEAD_DIM - 3 * NSA_HEADS, D_MODEL), F32)
    order = ("qa", "qb", "kb", "vb", "ks", "vs", "kw", "vw", "kc", "vc", "ga")
    p = {
        "wt": jnp.concatenate([part[n_] for n_ in order] + [ga_pad, part["gm"]], axis=0).astype(BF16),
        "gains": (gq_nsa[l], gq_moba[l], gk_moba[l], gk_sel[l], gk_win[l]),
    }
    w1kv, poskv, w2pad = [], [], []
    for w1, w2, pos in ((cmp_w1_k[l], cmp_w2_k[l], cmp_pos_k[l]), (cmp_w1_v[l], cmp_w2_v[l], cmp_pos_v[l])):
        w1r = w1.reshape(2, CMP_STRIDE, HEAD_DIM, CMP_HIDDEN)
        lohi = jnp.concatenate([w1r[0], w1r[1]], axis=-1)
        z = jnp.zeros_like(lohi)
        w1kv.append(jnp.stack([jnp.concatenate([lohi, z], axis=-1), jnp.concatenate([z, lohi], axis=-1)], axis=1)
                    .reshape(CMP_STRIDE * NSA_KV_WIDTH, NSA_KV_GROUPS * 2 * CMP_HIDDEN))
        posr = jnp.tile(pos.reshape(2, CMP_STRIDE, 1, HEAD_DIM), (1, 1, NSA_KV_GROUPS, 1))
        poskv.append(jnp.pad(posr.reshape(2, CMP_STRIDE * NSA_KV_WIDTH), ((0, 6), (0, 0))))
        zw = jnp.zeros_like(w2)
        w2pad += [jnp.concatenate([w2, zw], axis=1), jnp.concatenate([zw, w2], axis=1)]
    p["w1kv"] = jnp.stack(w1kv).astype(BF16)
    p["poskv"] = jnp.stack(poskv).astype(BF16)
    p["w2pad"] = jnp.stack(w2pad).astype(BF16)
    return p


def _rope_tables(pos):
    inv = ROPE_THETA ** (-jnp.arange(HALF, dtype=F32) / HALF)
    ang = inv[:, None] * pos.astype(F32)[None, :]
    return jnp.cos(ang), jnp.sin(ang)


def _cache_leaf(feat_major, heads):
    b, _, t = feat_major.shape
    return feat_major.reshape(b, 2, heads, HEAD_DIM, t).transpose(0, 4, 1, 2, 3)


def kernel(x_prompt, x_sample, cache_nsa_cmp, cache_nsa_sel, cache_moba, state_nsa_win, page_table,
           norm_mix, w_in, gq_nsa, gk_cmp, gk_sel, gk_win, gq_moba, gk_moba,
           cmp_pos_k, cmp_pos_v, cmp_w1_k, cmp_w2_k, cmp_w1_v, cmp_w2_v,
           w_up_nsa, w_up_moba, w_out, norm_mlp, w_mlp_up, w_mlp_down):
    depth = w_in.shape[0]
    b, t, d = x_prompt.shape
    n_dec = x_sample.shape[0]
    past_len = page_table.shape[1] * PAGE_SIZE

    cos_p, sin_p = _rope_tables(jnp.arange(t, dtype=jnp.int32))
    cos_s, sin_s = _rope_tables(jnp.full((n_dec,), past_len, dtype=jnp.int32))
    to_t = lambda c: c.transpose(0, 1, 3, 4, 5, 2)
    cmp_t, sel_t, moba_t, win_t = to_t(cache_nsa_cmp), to_t(cache_nsa_sel), to_t(cache_moba), to_t(state_nsa_win)

    xp = x_prompt
    xs = x_sample.reshape(1, n_dec, d)
    leaves = [[] for _ in range(8)]
    tm_p, tm_proj, tm_mlp = 512, 256, 1024
    kvw = NSA_KV_WIDTH
    for l in range(depth):
        p = _layer_params(l, w_in, gq_nsa, gk_sel, gk_win, gq_moba, gk_moba, cmp_pos_k, cmp_pos_v,
                          cmp_w1_k, cmp_w2_k, cmp_w1_v, cmp_w2_v)
        wa, wb, wo = w_up_nsa[l].astype(BF16), w_up_moba[l].astype(BF16), w_out[l].astype(BF16)
        w1, w2 = w_mlp_up[l].astype(BF16), w_mlp_down[l].astype(BF16)

        pr = _proj(xp, norm_mix[l], p["wt"], p["gains"], cos_p, sin_p, tm=tm_proj, attention_copies=True)
        kc, vc_t = _compress(pr["c_rows"], p["w1kv"], p["poskv"], p["w2pad"], gk_cmp[l])
        ks_blk, kw_blk = MOBA_WIDTH // kvw, MOBA_WIDTH // kvw + 1
        oa = _nsa(pr["qn_t"], pr["qr_t"], pr["ga_t"], kc, vc_t,
                  pr["k_rows"], ks_blk, pr["v_tt"], ks_blk, pr["k_rows"], kw_blk, pr["v_tt"], kw_blk)
        ob = _moba(pr["qb_t"], pr["k_rows"], 0, pr["v_tt"], 0)
        x2 = _merge(xp.reshape(b * t, d), oa.reshape(b * t, NSA_WIDTH), ob.reshape(b * t, MOBA_WIDTH),
                    pr["gm"], wa, wb, wo, tm=tm_p)
        xp = _mlp(x2, norm_mlp[l], w1, w2, tm=tm_mlp, tf=1024).reshape(b, t, d)
        leaves[0].append(_cache_leaf(pr["cmp_t"], NSA_KV_GROUPS))
        leaves[1].append(_cache_leaf(pr["sel_t"], NSA_KV_GROUPS))
        leaves[2].append(_cache_leaf(pr["moba_t"], MOBA_HEADS))
        leaves[3].append(_cache_leaf(pr["win_t"][:, :, t - min(WINDOW, t):], NSA_KV_GROUPS))

        sr = _proj(xs, norm_mix[l], p["wt"], p["gains"], cos_s, sin_s, tm=n_dec, attention_copies=False)
        seq_rows = lambda a: a[0].T.astype(F32)
        c_new, new_sel, new_win, new_moba = (seq_rows(sr[k_]) for k_ in ("cmp_t", "sel_t", "win_t", "moba_t"))
        q_rows = jnp.concatenate([seq_rows(sr["qn_t"]), seq_rows(sr["qr_t"]), seq_rows(sr["qb_t"])], axis=1)
        new_rows = jnp.concatenate([new_sel, new_win, new_moba], axis=1)
        ga_rows = jnp.pad(seq_rows(sr["ga_t"]), ((0, 0), (0, LANES - HEAD_DIM)))
        oa, ob = _decode(cmp_t, sel_t, moba_t, win_t, l, page_table, q_rows[:, None, :], new_rows[:, None, :],
                         ga_rows[:, None, :], p["w1kv"], p["poskv"], p["w2pad"], gk_cmp[l])
        x2 = _merge(xs.reshape(n_dec, d), oa[:, 0].astype(BF16), ob[:, 0].astype(BF16), sr["gm"], wa, wb, wo, tm=n_dec)
        xs = _mlp(x2, norm_mlp[l], w1, w2, tm=n_dec, tf=1024).reshape(1, n_dec, d)
        seq_leaf = lambda rows, heads: rows.reshape(n_dec, 1, 2, heads, HEAD_DIM)
        leaves[4].append(seq_leaf(c_new, NSA_KV_GROUPS))
        leaves[5].append(seq_leaf(new_sel, NSA_KV_GROUPS))
        leaves[6].append(seq_leaf(new_moba, MOBA_HEADS))
        win_all = jnp.concatenate([state_nsa_win[l], seq_leaf(new_win, NSA_KV_GROUPS)], axis=1)
        leaves[7].append(win_all[:, -min(WINDOW, past_len + 1):])
    return (xp, xs.reshape(n_dec, 1, d)) + tuple(jnp.stack(v) for v in leaves)
```

```python
import functools

import jax
import jax.numpy as jnp
from jax import lax
from jax.experimental import pallas as pl
from jax.experimental.pallas import tpu as pltpu

F32 = jnp.float32
BF16 = jnp.bfloat16

D_MODEL = 1024
HEAD_DIM = 64
HALF = HEAD_DIM // 2
NSA_HEADS = 8
NSA_KV_GROUPS = 2
NSA_REP = NSA_HEADS // NSA_KV_GROUPS
MOBA_HEADS = 8
NSA_WIDTH = NSA_HEADS * HEAD_DIM
NSA_KV_WIDTH = NSA_KV_GROUPS * HEAD_DIM
MOBA_WIDTH = MOBA_HEADS * HEAD_DIM
CMP_LEN = 32
CMP_STRIDE = 16
CMP_HIDDEN = 2 * HEAD_DIM
SEL_BLOCK = 64
SEL_TOPK = 16
WINDOW = 512
MOBA_BLOCK = 256
MOBA_TOPK = 3
D_FF = 4 * D_MODEL
PAGE_SIZE = 128
ROPE_THETA = 10000.0
NORM_EPS = 1e-6
NEG = -1e30
FORCED = 1e6
SCALE = HEAD_DIM ** -0.5

V7X_VMEM_BYTES = 64 * 1024 * 1024
VMEM_LIMIT = V7X_VMEM_BYTES - 8 * 1024 * 1024
KV_TILE = 512
NSA_Q_TILE = 128
MOBA_Q_TILE = MOBA_BLOCK


def _cparams(n_axes):
    return pltpu.CompilerParams(dimension_semantics=("arbitrary",) * n_axes,
                                vmem_limit_bytes=VMEM_LIMIT)


def _rms_rows(x, g):
    ms = jnp.mean(x * x, axis=-1, keepdims=True)
    return x * lax.rsqrt(ms + NORM_EPS) * g


def _sigmoid(x):
    return 1.0 / (1.0 + jnp.exp(-x))


_R_QA, _R_QB, _R_KB, _R_VB = 0, NSA_WIDTH, NSA_WIDTH + MOBA_WIDTH, NSA_WIDTH + 2 * MOBA_WIDTH
_R_KV4 = _R_VB + MOBA_WIDTH
_R_C = _R_KV4 + 4 * NSA_KV_WIDTH
_R_GA = _R_C + 2 * NSA_KV_WIDTH
_R_GM = _R_GA + HEAD_DIM
_R_END = _R_GM + 2 * D_MODEL
GM_CHUNK = 512


def _norm_rope_heads(acc, g_ref, cos, sin):
    nparts, fparts = [], []
    for hh in range(acc.shape[0] // HEAD_DIM):
        blk = acc[hh * HEAD_DIM:(hh + 1) * HEAD_DIM, :]
        ms = jnp.sum(blk * blk, axis=0, keepdims=True) * (1.0 / HEAD_DIM)
        y = blk * lax.rsqrt(ms + NORM_EPS) * g_ref[...]
        nparts.append(y)
        y1, y2 = y[:HALF, :], y[HALF:, :]
        fparts += [y1 * cos - y2 * sin, y2 * cos + y1 * sin]
    return jnp.concatenate(nparts, axis=0), jnp.concatenate(fparts, axis=0)


def _proj_kernel(x_ref, gn_ref, w_ref, gqa_ref, gqb_ref, gkb_ref, gks_ref, gkw_ref, cos_ref, sin_ref,
                 cmp_ref, sel_ref, win_ref, moba_ref, qn_ref, qr_ref, qb_ref, ga_ref, gm_ref, *copies):
    ht = _rms_rows(x_ref[0], gn_ref[...]).T.astype(BF16)
    cos, sin = cos_ref[...], sin_ref[...]
    kvw = NSA_KV_WIDTH
    mm = lambda r0, rows: jnp.dot(w_ref[r0:r0 + rows, :], ht, preferred_element_type=F32)
    c_rows_ref, k_rows_ref, v_tt_ref = copies if copies else (None, None, None)

    qn, qr = _norm_rope_heads(mm(_R_QA, NSA_WIDTH), gqa_ref, cos, sin)
    qn_ref[0] = qn.astype(BF16)
    qr_ref[0] = qr.astype(BF16)
    qb_ref[0] = _norm_rope_heads(mm(_R_QB, MOBA_WIDTH), gqb_ref, cos, sin)[1].astype(BF16)
    kb = _norm_rope_heads(mm(_R_KB, MOBA_WIDTH), gkb_ref, cos, sin)[1]
    vb = mm(_R_VB, MOBA_WIDTH)
    moba_ref[0, :MOBA_WIDTH, :] = kb
    moba_ref[0, MOBA_WIDTH:, :] = vb
    kv4 = mm(_R_KV4, 4 * kvw)
    ks = _norm_rope_heads(kv4[0:kvw], gks_ref, cos, sin)[1]
    kw = _norm_rope_heads(kv4[2 * kvw:3 * kvw], gkw_ref, cos, sin)[1]
    vs, vw = kv4[kvw:2 * kvw], kv4[3 * kvw:4 * kvw]
    sel_ref[0, :kvw, :] = ks
    sel_ref[0, kvw:, :] = vs
    win_ref[0, :kvw, :] = kw
    win_ref[0, kvw:, :] = vw
    c = mm(_R_C, 2 * kvw)
    cmp_ref[0] = c
    ga_ref[0] = _sigmoid(mm(_R_GA, HEAD_DIM))
    for k in range(2 * D_MODEL // GM_CHUNK):
        gm_ref[:, k * GM_CHUNK:(k + 1) * GM_CHUNK] = _sigmoid(mm(_R_GM + k * GM_CHUNK, GM_CHUNK)).T
    if copies:
        c_rows_ref[0] = c.T
        k_rows_ref[0, :, :MOBA_WIDTH] = kb.T.astype(BF16)
        k_rows_ref[0, :, MOBA_WIDTH:MOBA_WIDTH + kvw] = ks.T.astype(BF16)
        k_rows_ref[0, :, MOBA_WIDTH + kvw:] = kw.T.astype(BF16)
        v_tt_ref[0, 0, :MOBA_WIDTH, :] = vb.astype(BF16)
        v_tt_ref[0, 0, MOBA_WIDTH:MOBA_WIDTH + kvw, :] = vs.astype(BF16)
        v_tt_ref[0, 0, MOBA_WIDTH + kvw:, :] = vw.astype(BF16)


def _proj(x, gn, wt, gains, cos, sin, *, tm, attention_copies):
    b, t, d = x.shape
    kvw = NSA_KV_WIDTH
    n_pos_tiles = cos.shape[1] // tm
    tiles_per_kv = KV_TILE // tm if attention_copies else 1
    fixed2 = lambda bb, i: (0, 0)
    feat = lambda rows, dtype: (jax.ShapeDtypeStruct((b, rows, t), dtype),
                                pl.BlockSpec((1, rows, tm), lambda bb, i: (bb, 0, i)))
    outs = [feat(2 * kvw, F32), feat(2 * kvw, F32), feat(2 * kvw, F32), feat(2 * MOBA_WIDTH, F32),
            feat(NSA_WIDTH, BF16), feat(NSA_WIDTH, BF16), feat(MOBA_WIDTH, BF16), feat(HEAD_DIM, F32),
            (jax.ShapeDtypeStruct((b * t, 2 * d), F32),
             pl.BlockSpec((tm, 2 * d), lambda bb, i: (bb * (t // tm) + i, 0)))]
    if attention_copies:
        kcols = MOBA_WIDTH + 2 * kvw
        outs += [(jax.ShapeDtypeStruct((b, t, 2 * kvw), F32), pl.BlockSpec((1, tm, 2 * kvw), lambda bb, i: (bb, i, 0))),
                 (jax.ShapeDtypeStruct((b, t, kcols), BF16), pl.BlockSpec((1, tm, kcols), lambda bb, i: (bb, i, 0))),
                 (jax.ShapeDtypeStruct((b, t // KV_TILE, kcols, KV_TILE), BF16),
                  pl.BlockSpec((1, 1, kcols, tm), lambda bb, i: (bb, i // tiles_per_kv, 0, i % tiles_per_kv)))]
    res = pl.pallas_call(
        _proj_kernel,
        grid=(b, t // tm),
        in_specs=[pl.BlockSpec((1, tm, d), lambda bb, i: (bb, i, 0)),
                  pl.BlockSpec((1, d), fixed2),
                  pl.BlockSpec(wt.shape, fixed2)]
        + [pl.BlockSpec((HEAD_DIM, 1), fixed2)] * len(gains)
        + [pl.BlockSpec((HALF, tm), lambda bb, i: (0, i % n_pos_tiles))] * 2,
        out_specs=[o[1] for o in outs],
        out_shape=[o[0] for o in outs],
        compiler_params=_cparams(2),
        name="in_proj",
    )(x, gn.reshape(1, d), wt, *[g.reshape(HEAD_DIM, 1) for g in gains], cos, sin)
    names = ["cmp_t", "sel_t", "win_t", "moba_t", "qn_t", "qr_t", "qb_t", "ga_t", "gm", "c_rows", "k_rows", "v_tt"]
    return dict(zip(names, res))


def _gelu_tanh(x):
    return 0.5 * x * (1.0 + jnp.tanh(0.7978845608028654 * (x + 0.044715 * x * x * x)))


def _compress_blocks(load_rows, n_cmp, w1_ref, pos_ref, w2_ref, gk_ref):
    out = []
    for kv in range(2):
        sub = jnp.concatenate([load_rows(kv, j) for j in range(CMP_STRIDE)], axis=1).astype(BF16)
        hcat = jnp.dot(sub, w1_ref[kv], preferred_element_type=F32)
        bias = jnp.dot(pos_ref[kv], w1_ref[kv], preferred_element_type=F32)
        acc = None
        for g in range(NSA_KV_GROUPS):
            c0 = g * 2 * CMP_HIDDEN
            lo, hi = hcat[:, c0:c0 + CMP_HIDDEN], hcat[:, c0 + CMP_HIDDEN:c0 + 2 * CMP_HIDDEN]
            b = bias[0:1, c0:c0 + CMP_HIDDEN] + bias[1:2, c0 + CMP_HIDDEN:c0 + 2 * CMP_HIDDEN]
            hidden = lo + pltpu.roll(hi, n_cmp - 1, axis=0) + b
            part = jnp.dot(_gelu_tanh(hidden).astype(BF16), w2_ref[kv * NSA_KV_GROUPS + g],
                           preferred_element_type=F32)
            acc = part if acc is None else acc + part
        out.append(acc.T)
    parts = []
    for g in range(NSA_KV_GROUPS):
        blk = out[0][g * HEAD_DIM:(g + 1) * HEAD_DIM, :]
        ms = jnp.sum(blk * blk, axis=0, keepdims=True) * (1.0 / HEAD_DIM)
        parts.append(blk * lax.rsqrt(ms + NORM_EPS) * gk_ref[...])
    return jnp.concatenate(parts, axis=0), out[1]


def _compress_kernel(xk_ref, xv_ref, w1_ref, pos_ref, w2_ref, gk_ref, kc_ref, vct_ref):
    n_cmp = xk_ref.shape[1] // CMP_STRIDE
    x_refs = (xk_ref, xv_ref)
    kct, vct = _compress_blocks(lambda kv, j: x_refs[kv][0, pl.ds(j, n_cmp, stride=CMP_STRIDE), :],
                                n_cmp, w1_ref, pos_ref, w2_ref, gk_ref)
    kc_ref[0] = kct.T.astype(BF16)
    vct_ref[0] = vct.astype(BF16)


def _compress(c_rows, w1kv, poskv, w2pad, gk):
    nb, t, _ = c_rows.shape
    n_cmp = t // CMP_STRIDE
    full = lambda a: pl.BlockSpec(a.shape, lambda i: (0,) * a.ndim)
    return pl.pallas_call(
        _compress_kernel,
        grid=(nb,),
        in_specs=[pl.BlockSpec((1, t, NSA_KV_WIDTH), lambda i: (i, 0, 0)),
                  pl.BlockSpec((1, t, NSA_KV_WIDTH), lambda i: (i, 0, 1)),
                  full(w1kv), full(poskv), full(w2pad),
                  pl.BlockSpec((HEAD_DIM, 1), lambda i: (0, 0))],
        out_specs=[pl.BlockSpec((1, n_cmp, NSA_KV_WIDTH), lambda i: (i, 0, 0)),
                   pl.BlockSpec((1, NSA_KV_WIDTH, n_cmp), lambda i: (i, 0, 0))],
        out_shape=[jax.ShapeDtypeStruct((nb, n_cmp, NSA_KV_WIDTH), BF16),
                   jax.ShapeDtypeStruct((nb, NSA_KV_WIDTH, n_cmp), BF16)],
        compiler_params=_cparams(1),
        name="compress",
    )(c_rows, c_rows, w1kv, poskv, w2pad, gk.reshape(HEAD_DIM, 1))


M_FLOOR = -1e29


def _softmax_scores(s, m, acc, vt_aug):
    m_new = jnp.maximum(m, jnp.max(s, axis=0, keepdims=True))
    p = jnp.exp(s - m_new).astype(BF16)
    return m_new, jnp.exp(m - m_new) * acc + jnp.dot(vt_aug, p, preferred_element_type=F32)


def _rank_rows(score, blk):
    rank = jnp.zeros(score.shape, F32)
    for j in range(score.shape[0]):
        sj = score[j:j + 1, :]
        beats = (sj > score) | ((sj == score) & (blk > j))
        rank = rank + beats.astype(F32)
    return rank


def _rep_rows(row8, n_rows):
    return jnp.concatenate([row8] * (n_rows // 8), axis=0)


def _nsa_kernel(qn_ref, qr_ref, ga_ref, kc_ref, vct_ref, ks_ref, vst_ref, kw_ref, vwt_ref,
                o_ref, sel_ref, *, tq, tk, n_sel):
    qi = pl.program_id(1)
    q0 = qi * tq
    n = NSA_REP * tq
    n_cmp = kc_ref.shape[1]
    blocks_per_tile = tk // SEL_BLOCK
    lane = lax.broadcasted_iota(jnp.int32, (1, n), 1)
    t_lane = q0 + (lane & (tq - 1))
    t_q = q0 + lax.broadcasted_iota(jnp.int32, (1, tq), 1)
    cur = t_q >> 6
    blk = lax.broadcasted_iota(jnp.int32, (n_sel, 1), 0)
    cend = lax.broadcasted_iota(jnp.int32, (n_cmp, 1), 0) * CMP_STRIDE + (CMP_LEN - 1)
    krow = lax.broadcasted_iota(jnp.int32, (tk, 1), 0)
    ci = lax.broadcasted_iota(jnp.int32, (n_sel, n_cmp), 1)
    mi = lax.broadcasted_iota(jnp.int32, (n_sel, n_cmp), 0) * (SEL_BLOCK // CMP_STRIDE)
    amat = ((ci >= mi) & (ci <= mi + 3)).astype(F32) + ((ci >= mi - 1) & (ci <= mi + 2)).astype(F32)
    zeros_q = jnp.zeros((HEAD_DIM, n), BF16)
    ones_v = jnp.ones((HEAD_DIM, tk), BF16)

    def q_pad(ref, g):
        q = jnp.concatenate([ref[0, (g * NSA_REP + r) * HEAD_DIM:(g * NSA_REP + r + 1) * HEAD_DIM, :]
                             for r in range(NSA_REP)], axis=1) * SCALE
        return jnp.concatenate([q, zeros_q] if g == 0 else [zeros_q, q], axis=0)

    def gate_row(branch, g):
        return jnp.concatenate([ga_ref[0, branch * NSA_HEADS + g * NSA_REP + r:
                                       branch * NSA_HEADS + g * NSA_REP + r + 1, :]
                                for r in range(NSA_REP)], axis=1)

    groups = range(NSA_KV_GROUPS)
    g_rows = [slice(g * HEAD_DIM, (g + 1) * HEAD_DIM) for g in groups]
    qr = [q_pad(qr_ref, g) for g in groups]
    o_c = []
    for g in groups:
        rows = g_rows[g]
        qn = q_pad(qn_ref, g)
        sc = jnp.dot(kc_ref[0], qn, preferred_element_type=F32)
        mask_c = cend <= t_lane
        smc = jnp.where(mask_c, sc, NEG)
        e = jnp.where(mask_c, jnp.exp(smc - jnp.max(smc, axis=0, keepdims=True)), 0.0)
        den = jnp.sum(e, axis=0, keepdims=True)
        pc = e / jnp.where(den > 0.0, den, 1.0)
        o_c.append(jnp.dot(vct_ref[0], pc.astype(BF16), preferred_element_type=F32)[rows, :])
        imp_n = jnp.dot(amat, pc, preferred_element_type=F32, precision=lax.Precision.HIGHEST)
        imp = imp_n[:, 0:tq]
        for r in range(1, NSA_REP):
            imp = imp + imp_n[:, r * tq:(r + 1) * tq]
        forced = (blk == 0) | (blk >= cur - 1)
        score = jnp.where(blk > cur, NEG, jnp.where(forced, FORCED, imp))
        sel = (_rank_rows(score, blk) < SEL_TOPK) & (blk <= cur)
        sel_n = jnp.concatenate([jnp.where(sel, 0.0, NEG)] * NSA_REP, axis=1)
        for m_ in range(n_sel):
            sel_ref[g, m_] = jnp.broadcast_to(sel_n[m_:m_ + 1, :], (8, n))

    def tile(kv, carry, with_win, final):
        k0 = pl.multiple_of(kv * tk, tk)
        ks_t = ks_ref[0, pl.ds(k0, tk), :]
        raw = [jnp.dot(ks_t, qr[g], preferred_element_type=F32) for g in groups]
        if with_win:
            kw_t = kw_ref[0, pl.ds(k0, tk), :]
            raw += [jnp.dot(kw_t, qr[g], preferred_element_type=F32) for g in groups]
        out = []
        for g in groups:
            s = raw[g] + jnp.concatenate([_rep_rows(sel_ref[g, kv * blocks_per_tile + b_], SEL_BLOCK)
                                          for b_ in range(blocks_per_tile)], axis=0)
            if final:
                s = jnp.where((kv * tk + krow) <= t_lane, s, NEG)
            out.append(_softmax_scores(s, *carry[g], jnp.concatenate([vst_ref[0, kv, g_rows[g], :], ones_v], axis=0)))
        if not with_win:
            return tuple(out) + tuple(carry[NSA_KV_GROUPS:])
        rel = t_lane - (kv * tk + krow)
        keep = (rel >= 0) if final else (rel < WINDOW)
        for g in groups:
            s = jnp.where(keep, raw[NSA_KV_GROUPS + g], NEG)
            out.append(_softmax_scores(s, *carry[NSA_KV_GROUPS + g],
                                       jnp.concatenate([vwt_ref[0, kv, g_rows[g], :], ones_v], axis=0)))
        return tuple(out)

    init = tuple((jnp.full((1, n), M_FLOOR, F32), jnp.zeros((2 * HEAD_DIM, n), F32)) for _ in range(2 * NSA_KV_GROUPS))
    last = (q0 + tq - 1) // tk
    prev = jnp.maximum(last - 1, 0)
    carry = lax.fori_loop(0, prev, lambda kv, c: tile(kv, c, False, False), init)
    carry = lax.fori_loop(prev, last, lambda kv, c: tile(kv, c, True, False), carry)
    carry = tile(last, carry, True, True)

    for g in groups:
        acc_s, acc_w = carry[g][1], carry[NSA_KV_GROUPS + g][1]
        o_s = acc_s[:HEAD_DIM] / acc_s[HEAD_DIM:HEAD_DIM + 1]
        o_w = acc_w[:HEAD_DIM] / acc_w[HEAD_DIM:HEAD_DIM + 1]
        o = gate_row(0, g) * o_c[g] + gate_row(1, g) * o_s + gate_row(2, g) * o_w
        o_heads = jnp.concatenate([o[:, r * tq:(r + 1) * tq] for r in range(NSA_REP)], axis=0)
        o_ref[0, :, g * NSA_REP * HEAD_DIM:(g + 1) * NSA_REP * HEAD_DIM] = o_heads.T.astype(BF16)


def _nsa(qn_t, qr_t, ga_t, kc, vc_t, k_rows, ks_col, vs_tt, vs_row, kw_rows, kw_col, vw_tt, vw_row):
    b, _, tq_total = qn_t.shape
    tq, tk = NSA_Q_TILE, KV_TILE
    assert WINDOW == tk and tk % tq == 0 and tq_total % tk == 0
    n_cmp = kc.shape[1]
    l_sel = k_rows.shape[1]
    n_sel = ((l_sel // SEL_BLOCK) + 7) // 8 * 8
    l_win = kw_rows.shape[1]
    n = NSA_REP * tq
    return pl.pallas_call(
        functools.partial(_nsa_kernel, tq=tq, tk=tk, n_sel=n_sel),
        grid=(b, tq_total // tq),
        in_specs=[
            pl.BlockSpec((1, NSA_WIDTH, tq), lambda bb, i: (bb, 0, i)),
            pl.BlockSpec((1, NSA_WIDTH, tq), lambda bb, i: (bb, 0, i)),
            pl.BlockSpec((1, ga_t.shape[1], tq), lambda bb, i: (bb, 0, i)),
            pl.BlockSpec((1, n_cmp, NSA_KV_WIDTH), lambda bb, i: (bb, 0, 0)),
            pl.BlockSpec((1, NSA_KV_WIDTH, n_cmp), lambda bb, i: (bb, 0, 0)),
            pl.BlockSpec((1, l_sel, NSA_KV_WIDTH), lambda bb, i: (bb, 0, ks_col)),
            pl.BlockSpec((1, l_sel // tk, NSA_KV_WIDTH, tk), lambda bb, i: (bb, 0, vs_row, 0)),
            pl.BlockSpec((1, l_win, NSA_KV_WIDTH), lambda bb, i: (bb, 0, kw_col)),
            pl.BlockSpec((1, l_win // tk, NSA_KV_WIDTH, tk), lambda bb, i: (bb, 0, vw_row, 0)),
        ],
        out_specs=pl.BlockSpec((1, tq, NSA_WIDTH), lambda bb, i: (bb, i, 0)),
        out_shape=jax.ShapeDtypeStruct((b, tq_total, NSA_WIDTH), BF16),
        scratch_shapes=[pltpu.VMEM((NSA_KV_GROUPS, n_sel, 8, n), F32)],
        compiler_params=_cparams(2),
        name="nsa_attend",
    )(qn_t, qr_t, ga_t, kc, vc_t, k_rows, vs_tt, kw_rows, vw_tt)


MOBA_CHAINS = 4


def _moba_kernel(q_ref, k_ref, vt_ref, o_ref, km_ref, sel_ref, *, tq, tk, n_blk):
    qi = pl.program_id(2)
    q0 = qi * tq
    n = 2 * tq
    pair = 2 * HEAD_DIM
    cq = q0 // MOBA_BLOCK
    blocks_per_tile = tk // MOBA_BLOCK
    n_rows = km_ref.shape[1]

    @pl.when(qi == 0)
    def _():
        kall = k_ref[0].astype(F32)
        km = jnp.sum(kall.reshape(n_blk, MOBA_BLOCK, MOBA_CHAINS * pair), axis=1) * (1.0 / MOBA_BLOCK)
        if n_rows > n_blk:
            km = jnp.concatenate([km, jnp.zeros((n_rows - n_blk, MOBA_CHAINS * pair), F32)], axis=0)
        for c in range(MOBA_CHAINS):
            km_ref[c] = km[:, c * pair:(c + 1) * pair].astype(BF16)

    lane = lax.broadcasted_iota(jnp.int32, (1, n), 1)
    t_lane = q0 + (lane & (tq - 1))
    blk = lax.broadcasted_iota(jnp.int32, (n_rows, 1), 0)
    krow = lax.broadcasted_iota(jnp.int32, (MOBA_BLOCK, 1), 0)
    zq = jnp.zeros((HEAD_DIM, tq), BF16)
    ones_v = jnp.ones((16, tk), BF16)
    past = blk < cq
    qbds = []
    for c in range(MOBA_CHAINS):
        r0 = c * pair
        qbd = jnp.concatenate(
            [jnp.concatenate([q_ref[0, r0:r0 + HEAD_DIM, :] * SCALE, zq], axis=0),
             jnp.concatenate([zq, q_ref[0, r0 + HEAD_DIM:r0 + pair, :] * SCALE], axis=0)], axis=1)
        qbds.append(qbd)
        s_blk = jnp.dot(km_ref[c], qbd, preferred_element_type=F32) * (1.0 / SCALE)
        score = jnp.where(past, s_blk, NEG)
        sel_b = jnp.where((_rank_rows(score, blk) < MOBA_TOPK) & past, 0.0, NEG)
        for m_ in range(n_rows):
            sel_ref[c, m_] = jnp.broadcast_to(sel_b[m_:m_ + 1, :], (8, n))

    def tile(kv, carry, own_tile):
        raw = [jnp.dot(k_ref[0, pl.ds(pl.multiple_of(kv * tk, tk), tk), c * pair:(c + 1) * pair], qbds[c],
                       preferred_element_type=F32) for c in range(MOBA_CHAINS)]
        out = []
        for c in range(MOBA_CHAINS):
            m, acc = carry[c]
            parts = []
            for b_ in range(blocks_per_tile):
                nb = kv * blocks_per_tile + b_
                sb = raw[c][b_ * MOBA_BLOCK:(b_ + 1) * MOBA_BLOCK, :]
                picked = sb + _rep_rows(sel_ref[c, nb], MOBA_BLOCK)
                if own_tile:
                    own = jnp.where((nb * MOBA_BLOCK + krow) <= t_lane, sb, NEG)
                    picked = jnp.where(nb == cq, own, picked)
                parts.append(picked)
            vt_aug = jnp.concatenate([vt_ref[0, kv, c * pair:(c + 1) * pair, :], ones_v], axis=0)
            out.append(_softmax_scores(jnp.concatenate(parts, axis=0), m, acc, vt_aug))
        return tuple(out)

    init = tuple((jnp.full((1, n), M_FLOOR, F32), jnp.zeros((pair + 16, n), F32)) for _ in range(MOBA_CHAINS))
    last = (q0 + tq - 1) // tk
    carry = lax.fori_loop(0, last, lambda kv, cr: tile(kv, cr, False), init)
    carry = tile(last, carry, True)
    for c in range(MOBA_CHAINS):
        acc = carry[c][1]
        o = acc[:pair] / acc[pair:pair + 1]
        o2 = jnp.concatenate([o[0:HEAD_DIM, 0:tq], o[HEAD_DIM:pair, tq:2 * tq]], axis=0)
        o_ref[0, :, c * pair:(c + 1) * pair] = o2.T.astype(BF16)


def _moba(q_t, k_rows, k_col0, v_tt, v_row0):
    b, _, tq_total = q_t.shape
    tq, tk = MOBA_Q_TILE, KV_TILE
    width = MOBA_CHAINS * 2 * HEAD_DIM
    assert tq == MOBA_BLOCK and tk % tq == 0 and k_col0 % width == 0 and v_row0 % width == 0
    l_kv = k_rows.shape[1]
    n_blk = l_kv // MOBA_BLOCK
    n_rows = (n_blk + 7) // 8 * 8
    n = 2 * tq
    return pl.pallas_call(
        functools.partial(_moba_kernel, tq=tq, tk=tk, n_blk=n_blk),
        grid=(b, MOBA_WIDTH // width, tq_total // tq),
        in_specs=[
            pl.BlockSpec((1, width, tq), lambda bb, hp, i: (bb, hp, i)),
            pl.BlockSpec((1, l_kv, width), lambda bb, hp, i: (bb, 0, k_col0 // width + hp)),
            pl.BlockSpec((1, l_kv // tk, width, tk), lambda bb, hp, i: (bb, 0, v_row0 // width + hp, 0)),
        ],
        out_specs=pl.BlockSpec((1, tq, width), lambda bb, hp, i: (bb, i, hp)),
        out_shape=jax.ShapeDtypeStruct((b, tq_total, MOBA_WIDTH), BF16),
        scratch_shapes=[pltpu.VMEM((MOBA_CHAINS, n_rows, 2 * HEAD_DIM), BF16),
                        pltpu.VMEM((MOBA_CHAINS, n_rows, 8, n), F32)],
        compiler_params=_cparams(3),
        name="moba_attend",
    )(q_t, k_rows, v_tt)


def _merge_kernel(x_ref, oa_ref, ob_ref, gm_ref, wa_ref, wb_ref, wo_ref, o_ref):
    ua = jnp.dot(oa_ref[...], wa_ref[...], preferred_element_type=F32)
    ub = jnp.dot(ob_ref[...], wb_ref[...], preferred_element_type=F32)
    mixed = gm_ref[:, :D_MODEL] * ua + gm_ref[:, D_MODEL:] * ub
    o_ref[...] = x_ref[...] + jnp.dot(mixed.astype(BF16), wo_ref[...], preferred_element_type=F32)


def _merge(x2, oa, ob, gm, wa, wb, wo, *, tm):
    m, d = x2.shape
    row = lambda i: (i, 0)
    fixed = lambda i: (0, 0)
    return pl.pallas_call(
        _merge_kernel,
        grid=(m // tm,),
        in_specs=[pl.BlockSpec((tm, d), row), pl.BlockSpec((tm, NSA_WIDTH), row),
                  pl.BlockSpec((tm, MOBA_WIDTH), row), pl.BlockSpec((tm, 2 * d), row),
                  pl.BlockSpec(wa.shape, fixed), pl.BlockSpec(wb.shape, fixed), pl.BlockSpec(wo.shape, fixed)],
        out_specs=pl.BlockSpec((tm, d), row),
        out_shape=jax.ShapeDtypeStruct((m, d), F32),
        compiler_params=_cparams(1),
        name="merge_out",
    )(x2, oa, ob, gm, wa, wb, wo)


def _mlp_kernel(x_ref, g_ref, w1_ref, w2_ref, o_ref, h_ref, acc_ref):
    c = pl.program_id(1)

    @pl.when(c == 0)
    def _():
        h_ref[...] = _rms_rows(x_ref[...], g_ref[...]).astype(BF16)
        acc_ref[...] = x_ref[...]

    u = jnp.maximum(jnp.dot(h_ref[...], w1_ref[...].astype(BF16), preferred_element_type=F32), 0.0)
    acc_ref[...] += jnp.dot((u * u).astype(BF16), w2_ref[...].astype(BF16), preferred_element_type=F32)

    @pl.when(c == pl.num_programs(1) - 1)
    def _():
        o_ref[...] = acc_ref[...]


def _mlp(x2, g, w1_all, w2_all, layer, *, tm, tf):
    m, d = x2.shape
    dff = w1_all.shape[2]
    return pl.pallas_call(
        _mlp_kernel,
        grid=(m // tm, dff // tf),
        in_specs=[pl.BlockSpec((tm, d), lambda i, c: (i, 0)),
                  pl.BlockSpec((1, d), lambda i, c: (0, 0)),
                  pl.BlockSpec((None, d, tf), lambda i, c: (layer, 0, c)),
                  pl.BlockSpec((None, tf, d), lambda i, c: (layer, c, 0))],
        out_specs=pl.BlockSpec((tm, d), lambda i, c: (i, 0)),
        out_shape=jax.ShapeDtypeStruct((m, d), F32),
        scratch_shapes=[pltpu.VMEM((tm, d), BF16), pltpu.VMEM((tm, d), F32)],
        compiler_params=_cparams(2),
        name="sq_relu_mlp",
    )(x2, g.reshape(1, d), w1_all, w2_all)


LANES = 128


def _stack_rows(rows, n_rows=8):
    idx = lax.broadcasted_iota(jnp.int32, (n_rows, 1), 0)
    out = jnp.zeros((n_rows, rows[0].shape[1]), F32)
    for i, r in enumerate(rows):
        out = jnp.where(idx == i, r, out)
    return out


def _lane_col(x, j, lane):
    return jnp.sum(jnp.where(lane == j, x, 0.0), axis=1, keepdims=True)


def _heads_on_rows(q_row, n_heads, feats):
    rows8 = lax.broadcasted_iota(jnp.int32, (8, feats), 0)
    lanes = lax.broadcasted_iota(jnp.int32, (8, feats), 1)
    if feats == n_heads * HEAD_DIM:
        return jnp.where((lanes // HEAD_DIM) == rows8, jnp.broadcast_to(q_row, (8, feats)), 0.0)
    rows = []
    for h in range(n_heads):
        c, off, g = h // 2, (h % 2) * HEAD_DIM, h // NSA_REP
        r = q_row[:, c * LANES:(c + 1) * LANES]
        rows.append(r if off == g * HEAD_DIM else pltpu.roll(r, HEAD_DIM, axis=1))
    q8 = _stack_rows(rows)
    return jnp.where((lanes // HEAD_DIM) == (rows8 // NSA_REP), q8, 0.0)


def _attend_pages(q8, kts, vts, masks, k_new, v_new, new_ok):
    qb = q8.astype(BF16)
    s = [jnp.where(masks[p], jnp.dot(qb, kts[p].astype(BF16), preferred_element_type=F32), NEG)
         for p in range(len(kts))]
    s_new = jnp.where(new_ok, jnp.sum(q8 * k_new, axis=1, keepdims=True), NEG)
    m = s[0]
    for sp in s[1:]:
        m = jnp.maximum(m, sp)
    m = jnp.maximum(jnp.max(m, axis=1, keepdims=True), s_new)
    e_new = jnp.where(new_ok, jnp.exp(s_new - m), 0.0)
    den = e_new
    acc = e_new * v_new
    for p in range(len(kts)):
        e = jnp.where(masks[p], jnp.exp(s[p] - m), 0.0)
        den = den + jnp.sum(e, axis=1, keepdims=True)
        acc = acc + lax.dot_general(e.astype(BF16), vts[p].astype(BF16), (((1,), (1,)), ((), ())),
                                    preferred_element_type=F32)
    return acc / den


def _decode_kernel(pt_ref, *refs, n_pages, t_pos, w_buf):
    del pt_ref
    cmp_pages = refs[:n_pages]
    sel_pages = refs[n_pages:2 * n_pages]
    moba_pages = refs[2 * n_pages:3 * n_pages]
    (win_ref, q_ref, new_ref, ga_ref, w1_ref, pos_ref, w2_ref, gk_ref,
     oa_ref, ob_ref, xk_ref, xv_ref) = refs[3 * n_pages:]
    past = n_pages * PAGE_SIZE
    n_cmp = past // CMP_STRIDE
    lane = lax.broadcasted_iota(jnp.int32, (1, LANES), 1)
    sub_i = lax.broadcasted_iota(jnp.int32, (LANES, 1), 0)
    row8 = lax.broadcasted_iota(jnp.int32, (8, 1), 0)

    kvw = NSA_KV_WIDTH
    qn8 = _heads_on_rows(q_ref[0, :, 0:NSA_WIDTH], NSA_HEADS, kvw) * SCALE
    qr8 = _heads_on_rows(q_ref[0, :, NSA_WIDTH:2 * NSA_WIDTH], NSA_HEADS, kvw) * SCALE
    qb8 = _heads_on_rows(q_ref[0, :, 2 * NSA_WIDTH:2 * NSA_WIDTH + MOBA_WIDTH], MOBA_HEADS, MOBA_WIDTH)
    new = lambda off, width: new_ref[0, :, off:off + width]
    ks_new, vs_new, kw_new, vw_new = new(0, kvw), new(kvw, kvw), new(2 * kvw, kvw), new(3 * kvw, kvw)
    kb_new, vb_new = new(4 * kvw, MOBA_WIDTH), new(4 * kvw + MOBA_WIDTH, MOBA_WIDTH)
    ga_cols = jnp.broadcast_to(ga_ref[0], (LANES, LANES)).T
    gate = [ga_cols[br * NSA_HEADS:(br + 1) * NSA_HEADS, 0:1] for br in range(3)]
    grp8 = row8 // NSA_REP
    mpage = lambda p, kv: moba_pages[p][kv].reshape(MOBA_WIDTH, PAGE_SIZE).astype(BF16)
    qbb = qb8.astype(BF16)
    raw = [jnp.dot(qbb, mpage(p, 0), preferred_element_type=F32) for p in range(n_pages)]

    x_refs = (xk_ref, xv_ref)
    for p in range(n_pages):
        for kv in range(2):
            x_refs[kv][p * PAGE_SIZE:(p + 1) * PAGE_SIZE, :] = cmp_pages[p][kv].reshape(kvw, PAGE_SIZE).T
    ck_t = _compress_blocks(lambda kv, j: x_refs[kv][pl.ds(j, n_cmp, stride=CMP_STRIDE), :],
                            n_cmp, w1_ref, pos_ref, w2_ref, gk_ref)

    cend_ok = (lane * CMP_STRIDE + (CMP_LEN - 1)) <= t_pos
    cur = t_pos // SEL_BLOCK
    forced = (lane == 0) | (lane >= cur - 1)
    mi = lane * (SEL_BLOCK // CMP_STRIDE)
    amat_t = (((sub_i >= mi) & (sub_i <= mi + 3)).astype(F32)
              + ((sub_i >= mi - 1) & (sub_i <= mi + 2)).astype(F32))
    blocks_per_page = PAGE_SIZE // SEL_BLOCK

    kct, vct = ck_t
    sm = jnp.where(cend_ok, jnp.dot(qn8.astype(BF16), kct.astype(BF16), preferred_element_type=F32), NEG)
    e = jnp.where(cend_ok, jnp.exp(sm - jnp.max(sm, axis=1, keepdims=True)), 0.0)
    den = jnp.sum(e, axis=1, keepdims=True)
    pc = e / jnp.where(den > 0.0, den, 1.0)
    o_c = lax.dot_general(pc.astype(BF16), vct.astype(BF16), (((1,), (1,)), ((), ())), preferred_element_type=F32)
    sel_rows = []
    for g in range(NSA_KV_GROUPS):
        psum = jnp.broadcast_to(jnp.sum(jnp.where(grp8 == g, pc, 0.0), axis=0, keepdims=True), (8, LANES))
        imp = jnp.dot(psum, amat_t, preferred_element_type=F32, precision=lax.Precision.HIGHEST)[0:1, :]
        score = jnp.where(lane > cur, NEG, jnp.where(forced, FORCED, imp))
        rm = jnp.broadcast_to(score, (LANES, LANES))
        cm = rm.T
        rank = jnp.sum(((cm > rm) | ((cm == rm) & (sub_i < lane))).astype(F32), axis=0, keepdims=True)
        sel_rows.append(((rank < SEL_TOPK) & (lane <= cur)).astype(F32))
    sel8 = jnp.where(grp8 == 0, sel_rows[0], sel_rows[1])
    masks = []
    for p in range(n_pages):
        picks = [_lane_col(sel8, p * blocks_per_page + b_, lane) for b_ in range(blocks_per_page)]
        mrow = picks[-1]
        for b_ in range(blocks_per_page - 2, -1, -1):
            mrow = jnp.where(lane < (b_ + 1) * SEL_BLOCK, picks[b_], mrow)
        masks.append(mrow > 0.5)
    page2d = lambda ref, kv: ref[kv].reshape(kvw, ref.shape[-1])
    o_s = _attend_pages(qr8, [page2d(sel_pages[p], 0) for p in range(n_pages)],
                        [page2d(sel_pages[p], 1) for p in range(n_pages)], masks,
                        ks_new, vs_new, _lane_col(sel8, cur, lane) > 0.5)
    wmasks = []
    for c in range(w_buf // LANES):
        rel = t_pos - (past - w_buf + c * LANES + lane)
        wmasks.append((rel >= 0) & (rel < WINDOW))
    win_k, win_v = page2d(win_ref, 0), page2d(win_ref, 1)
    o_w = _attend_pages(qr8, [win_k[:, c * LANES:(c + 1) * LANES] for c in range(w_buf // LANES)],
                        [win_v[:, c * LANES:(c + 1) * LANES] for c in range(w_buf // LANES)], wmasks,
                        kw_new, vw_new, row8 >= 0)
    o8 = gate[0] * o_c + gate[1] * o_s + gate[2] * o_w
    for c in range(NSA_HEADS // 2):
        even, odd = o8[2 * c:2 * c + 1, :], o8[2 * c + 1:2 * c + 2, :]
        if (2 * c) // NSA_REP == 0:
            odd = pltpu.roll(odd, HEAD_DIM, axis=1)
        else:
            even = pltpu.roll(even, HEAD_DIM, axis=1)
        oa_ref[0, :, c * LANES:(c + 1) * LANES] = jnp.where(lane < HEAD_DIM, even, odd)

    pages_per_blk = MOBA_BLOCK // PAGE_SIZE
    cq = t_pos // MOBA_BLOCK
    s_blk = jnp.zeros((MOBA_HEADS, LANES), F32)
    for n_ in range(past // MOBA_BLOCK):
        tot = raw[n_ * pages_per_blk]
        for k_ in range(1, pages_per_blk):
            tot = tot + raw[n_ * pages_per_blk + k_]
        s_blk = jnp.where(lane == n_, jnp.sum(tot, axis=1, keepdims=True) * (1.0 / MOBA_BLOCK), s_blk)
    past_m = lane < cq
    score = jnp.where(past_m, s_blk, NEG)
    rank = jnp.zeros((MOBA_HEADS, LANES), F32)
    for j in range(cq):
        cj = _lane_col(score, j, lane)
        rank = rank + ((cj > score) | ((cj == score) & (lane > j))).astype(F32)
    sel = (past_m & (rank < MOBA_TOPK)).astype(F32)
    picked = [_lane_col(sel, n_, lane) > 0.5 for n_ in range(past // MOBA_BLOCK)]
    s_new = jnp.sum(qb8 * kb_new, axis=1, keepdims=True) * SCALE
    sc = [jnp.where(picked[p // pages_per_blk], raw[p] * SCALE, NEG) for p in range(n_pages)]
    m = sc[0]
    for sp in sc[1:]:
        m = jnp.maximum(m, sp)
    m = jnp.maximum(jnp.max(m, axis=1, keepdims=True), s_new)
    e_new = jnp.exp(s_new - m)
    den = e_new
    acc = e_new * vb_new
    for p in range(n_pages):
        e = jnp.where(picked[p // pages_per_blk], jnp.exp(sc[p] - m), 0.0)
        den = den + jnp.sum(e, axis=1, keepdims=True)
        acc = acc + lax.dot_general(e.astype(BF16), mpage(p, 1), (((1,), (1,)), ((), ())),
                                    preferred_element_type=F32)
    rows8w = lax.broadcasted_iota(jnp.int32, (MOBA_HEADS, MOBA_WIDTH), 0)
    lanes_w = lax.broadcasted_iota(jnp.int32, (MOBA_HEADS, MOBA_WIDTH), 1)
    ob_ref[0] = jnp.sum(jnp.where((lanes_w // HEAD_DIM) == rows8w, acc / den, 0.0), axis=0, keepdims=True)


def _decode(cmp_t, sel_t, moba_t, win_t, layer, page_table, q_rows, new_rows, ga_rows, w1kv, poskv, w2pad, gk):
    n_seq, n_pages = page_table.shape
    past = n_pages * PAGE_SIZE
    w_buf = win_t.shape[-1]
    assert past // CMP_STRIDE == LANES and past % MOBA_BLOCK == 0 and w_buf % LANES == 0

    def page_spec(h, p):
        return pl.BlockSpec((None, None, 2, h, HEAD_DIM, PAGE_SIZE), lambda s, pt: (layer, pt[s, p], 0, 0, 0, 0))

    row_spec = lambda a: pl.BlockSpec((1, 1, a.shape[2]), lambda s, pt: (s, 0, 0))
    full = lambda a: pl.BlockSpec(a.shape, lambda s, pt: (0,) * a.ndim)
    in_specs = ([page_spec(NSA_KV_GROUPS, p) for p in range(n_pages)]
                + [page_spec(NSA_KV_GROUPS, p) for p in range(n_pages)]
                + [page_spec(MOBA_HEADS, p) for p in range(n_pages)]
                + [pl.BlockSpec((None, None, 2, NSA_KV_GROUPS, HEAD_DIM, w_buf), lambda s, pt: (layer, s, 0, 0, 0, 0)),
                   row_spec(q_rows), row_spec(new_rows), row_spec(ga_rows),
                   full(w1kv), full(poskv), full(w2pad), pl.BlockSpec((HEAD_DIM, 1), lambda s, pt: (0, 0))])
    out_spec = pl.BlockSpec((1, 1, NSA_WIDTH), lambda s, pt: (s, 0, 0))
    return pl.pallas_call(
        functools.partial(_decode_kernel, n_pages=n_pages, t_pos=past, w_buf=w_buf),
        grid_spec=pltpu.PrefetchScalarGridSpec(
            num_scalar_prefetch=1, grid=(n_seq,), in_specs=in_specs, out_specs=[out_spec, out_spec],
            scratch_shapes=[pltpu.VMEM((past, NSA_KV_WIDTH), F32)] * 2),
        out_shape=[jax.ShapeDtypeStruct((n_seq, 1, NSA_WIDTH), F32)] * 2,
        compiler_params=_cparams(1),
        name="decode_mixers",
    )(page_table, *([cmp_t] * n_pages), *([sel_t] * n_pages), *([moba_t] * n_pages), win_t,
      q_rows, new_rows, ga_rows, w1kv, poskv, w2pad, gk.reshape(HEAD_DIM, 1))


_SPLITS = (NSA_WIDTH, NSA_KV_WIDTH, NSA_KV_WIDTH, NSA_KV_WIDTH, NSA_KV_WIDTH, NSA_KV_WIDTH, NSA_KV_WIDTH,
           3 * NSA_HEADS, MOBA_WIDTH, MOBA_WIDTH, MOBA_WIDTH, 2 * D_MODEL)
_NAMES = ("qa", "kc", "vc", "ks", "vs", "kw", "vw", "ga", "qb", "kb", "vb", "gm")


def _layer_params(l, w_in, gq_nsa, gk_sel, gk_win, gq_moba, gk_moba, cmp_pos_k, cmp_pos_v,
                  cmp_w1_k, cmp_w2_k, cmp_w1_v, cmp_w2_v):
    wt = w_in[l].T
    off, part = 0, {}
    for name, size in zip(_NAMES, _SPLITS):
        part[name] = wt[off:off + size]
        off += size
    ga_pad = jnp.zeros((HEAD_DIM - 3 * NSA_HEADS, D_MODEL), F32)
    order = ("qa", "qb", "kb", "vb", "ks", "vs", "kw", "vw", "kc", "vc", "ga")
    p = {
        "wt": jnp.concatenate([part[n_] for n_ in order] + [ga_pad, part["gm"]], axis=0).astype(BF16),
        "gains": (gq_nsa[l], gq_moba[l], gk_moba[l], gk_sel[l], gk_win[l]),
    }
    w1kv, poskv, w2pad = [], [], []
    for w1, w2, pos in ((cmp_w1_k[l], cmp_w2_k[l], cmp_pos_k[l]), (cmp_w1_v[l], cmp_w2_v[l], cmp_pos_v[l])):
        w1r = w1.reshape(2, CMP_STRIDE, HEAD_DIM, CMP_HIDDEN)
        lohi = jnp.concatenate([w1r[0], w1r[1]], axis=-1)
        z = jnp.zeros_like(lohi)
        w1kv.append(jnp.stack([jnp.concatenate([lohi, z], axis=-1), jnp.concatenate([z, lohi], axis=-1)], axis=1)
                    .reshape(CMP_STRIDE * NSA_KV_WIDTH, NSA_KV_GROUPS * 2 * CMP_HIDDEN))
        posr = jnp.tile(pos.reshape(2, CMP_STRIDE, 1, HEAD_DIM), (1, 1, NSA_KV_GROUPS, 1))
        poskv.append(jnp.pad(posr.reshape(2, CMP_STRIDE * NSA_KV_WIDTH), ((0, 6), (0, 0))))
        zw = jnp.zeros_like(w2)
        w2pad += [jnp.concatenate([w2, zw], axis=1), jnp.concatenate([zw, w2], axis=1)]
    p["w1kv"] = jnp.stack(w1kv).astype(BF16)
    p["poskv"] = jnp.stack(poskv).astype(BF16)
    p["w2pad"] = jnp.stack(w2pad).astype(BF16)
    return p


def _rope_tables(pos):
    inv = ROPE_THETA ** (-jnp.arange(HALF, dtype=F32) / HALF)
    ang = inv[:, None] * pos.astype(F32)[None, :]
    return jnp.cos(ang), jnp.sin(ang)


def _cache_leaf(feat_major, heads):
    b, _, t = feat_major.shape
    return feat_major.reshape(b, 2, heads, HEAD_DIM, t).transpose(0, 4, 1, 2, 3)


def kernel(x_prompt, x_sample, cache_nsa_cmp, cache_nsa_sel, cache_moba, state_nsa_win, page_table,
           norm_mix, w_in, gq_nsa, gk_cmp, gk_sel, gk_win, gq_moba, gk_moba,
           cmp_pos_k, cmp_pos_v, cmp_w1_k, cmp_w2_k, cmp_w1_v, cmp_w2_v,
           w_up_nsa, w_up_moba, w_out, norm_mlp, w_mlp_up, w_mlp_down):
    depth = w_in.shape[0]
    b, t, d = x_prompt.shape
    n_dec = x_sample.shape[0]
    past_len = page_table.shape[1] * PAGE_SIZE

    cos_p, sin_p = _rope_tables(jnp.arange(t, dtype=jnp.int32))
    cos_s, sin_s = _rope_tables(jnp.full((n_dec,), past_len, dtype=jnp.int32))
    to_t = lambda c: c.transpose(0, 1, 3, 4, 5, 2)
    cmp_t, sel_t, moba_t, win_t = to_t(cache_nsa_cmp), to_t(cache_nsa_sel), to_t(cache_moba), to_t(state_nsa_win)

    xp = x_prompt
    xs = x_sample.reshape(1, n_dec, d)
    leaves = [[] for _ in range(8)]
    tm_p, tm_proj, tm_mlp = 512, 256, 1024
    kvw = NSA_KV_WIDTH
    for l in range(depth):
        p = _layer_params(l, w_in, gq_nsa, gk_sel, gk_win, gq_moba, gk_moba, cmp_pos_k, cmp_pos_v,
                          cmp_w1_k, cmp_w2_k, cmp_w1_v, cmp_w2_v)
        wa, wb, wo = w_up_nsa[l].astype(BF16), w_up_moba[l].astype(BF16), w_out[l].astype(BF16)

        pr = _proj(xp, norm_mix[l], p["wt"], p["gains"], cos_p, sin_p, tm=tm_proj, attention_copies=True)
        kc, vc_t = _compress(pr["c_rows"], p["w1kv"], p["poskv"], p["w2pad"], gk_cmp[l])
        ks_blk, kw_blk = MOBA_WIDTH // kvw, MOBA_WIDTH // kvw + 1
        oa = _nsa(pr["qn_t"], pr["qr_t"], pr["ga_t"], kc, vc_t,
                  pr["k_rows"], ks_blk, pr["v_tt"], ks_blk, pr["k_rows"], kw_blk, pr["v_tt"], kw_blk)
        ob = _moba(pr["qb_t"], pr["k_rows"], 0, pr["v_tt"], 0)
        x2 = _merge(xp.reshape(b * t, d), oa.reshape(b * t, NSA_WIDTH), ob.reshape(b * t, MOBA_WIDTH),
                    pr["gm"], wa, wb, wo, tm=tm_p)
        xp = _mlp(x2, norm_mlp[l], w_mlp_up, w_mlp_down, l, tm=tm_mlp, tf=1024).reshape(b, t, d)
        leaves[0].append(_cache_leaf(pr["cmp_t"], NSA_KV_GROUPS))
        leaves[1].append(_cache_leaf(pr["sel_t"], NSA_KV_GROUPS))
        leaves[2].append(_cache_leaf(pr["moba_t"], MOBA_HEADS))
        leaves[3].append(_cache_leaf(pr["win_t"][:, :, t - min(WINDOW, t):], NSA_KV_GROUPS))

        sr = _proj(xs, norm_mix[l], p["wt"], p["gains"], cos_s, sin_s, tm=n_dec, attention_copies=False)
        seq_rows = lambda a: a[0].T.astype(F32)
        c_new, new_sel, new_win, new_moba = (seq_rows(sr[k_]) for k_ in ("cmp_t", "sel_t", "win_t", "moba_t"))
        q_rows = jnp.concatenate([seq_rows(sr["qn_t"]), seq_rows(sr["qr_t"]), seq_rows(sr["qb_t"])], axis=1)
        new_rows = jnp.concatenate([new_sel, new_win, new_moba], axis=1)
        ga_rows = jnp.pad(seq_rows(sr["ga_t"]), ((0, 0), (0, LANES - HEAD_DIM)))
        oa, ob = _decode(cmp_t, sel_t, moba_t, win_t, l, page_table, q_rows[:, None, :], new_rows[:, None, :],
                         ga_rows[:, None, :], p["w1kv"], p["poskv"], p["w2pad"], gk_cmp[l])
        x2 = _merge(xs.reshape(n_dec, d), oa[:, 0].astype(BF16), ob[:, 0].astype(BF16), sr["gm"], wa, wb, wo, tm=n_dec)
        xs = _mlp(x2, norm_mlp[l], w_mlp_up, w_mlp_down, l, tm=n_dec, tf=1024).reshape(1, n_dec, d)
        seq_leaf = lambda rows, heads: rows.reshape(n_dec, 1, 2, heads, HEAD_DIM)
        leaves[4].append(seq_leaf(c_new, NSA_KV_GROUPS))
        leaves[5].append(seq_leaf(new_sel, NSA_KV_GROUPS))
        leaves[6].append(seq_leaf(new_moba, MOBA_HEADS))
        win_all = jnp.concatenate([state_nsa_win[l], seq_leaf(new_win, NSA_KV_GROUPS)], axis=1)
        leaves[7].append(win_all[:, -min(WINDOW, past_len + 1):])
    return (xp, xs.reshape(n_dec, 1, d)) + tuple(jnp.stack(v) for v in leaves)
```

```python
import functools

import jax
import jax.numpy as jnp
from jax import lax
from jax.experimental import pallas as pl
from jax.experimental.pallas import tpu as pltpu

F32 = jnp.float32
BF16 = jnp.bfloat16

D_MODEL = 1024
HEAD_DIM = 64
HALF = HEAD_DIM // 2
NSA_HEADS = 8
NSA_KV_GROUPS = 2
NSA_REP = NSA_HEADS // NSA_KV_GROUPS
MOBA_HEADS = 8
NSA_WIDTH = NSA_HEADS * HEAD_DIM
NSA_KV_WIDTH = NSA_KV_GROUPS * HEAD_DIM
MOBA_WIDTH = MOBA_HEADS * HEAD_DIM
CMP_LEN = 32
CMP_STRIDE = 16
CMP_HIDDEN = 2 * HEAD_DIM
SEL_BLOCK = 64
SEL_TOPK = 16
WINDOW = 512
MOBA_BLOCK = 256
MOBA_TOPK = 3
D_FF = 4 * D_MODEL
PAGE_SIZE = 128
ROPE_THETA = 10000.0
NORM_EPS = 1e-6
NEG = -1e30
FORCED = 1e6
SCALE = HEAD_DIM ** -0.5

V7X_VMEM_BYTES = 64 * 1024 * 1024
VMEM_LIMIT = V7X_VMEM_BYTES - 8 * 1024 * 1024
KV_TILE = 512
NSA_Q_TILE = 256
MOBA_Q_TILE = MOBA_BLOCK


def _cparams(n_axes):
    return pltpu.CompilerParams(dimension_semantics=("arbitrary",) * n_axes,
                                vmem_limit_bytes=VMEM_LIMIT)


def _rms_rows(x, g):
    ms = jnp.mean(x * x, axis=-1, keepdims=True)
    return x * lax.rsqrt(ms + NORM_EPS) * g


def _sigmoid(x):
    return 1.0 / (1.0 + jnp.exp(-x))


_R_QA, _R_QB, _R_KB, _R_VB = 0, NSA_WIDTH, NSA_WIDTH + MOBA_WIDTH, NSA_WIDTH + 2 * MOBA_WIDTH
_R_KV4 = _R_VB + MOBA_WIDTH
_R_C = _R_KV4 + 4 * NSA_KV_WIDTH
_R_GA = _R_C + 2 * NSA_KV_WIDTH
_R_GM = _R_GA + HEAD_DIM
_R_END = _R_GM + 2 * D_MODEL
GM_CHUNK = 512


def _norm_rope_heads(acc, g_ref, cos, sin):
    nparts, fparts = [], []
    for hh in range(acc.shape[0] // HEAD_DIM):
        blk = acc[hh * HEAD_DIM:(hh + 1) * HEAD_DIM, :]
        ms = jnp.sum(blk * blk, axis=0, keepdims=True) * (1.0 / HEAD_DIM)
        y = blk * lax.rsqrt(ms + NORM_EPS) * g_ref[...]
        nparts.append(y)
        y1, y2 = y[:HALF, :], y[HALF:, :]
        fparts += [y1 * cos - y2 * sin, y2 * cos + y1 * sin]
    return jnp.concatenate(nparts, axis=0), jnp.concatenate(fparts, axis=0)


def _proj_kernel(x_ref, gn_ref, w_ref, gqa_ref, gqb_ref, gkb_ref, gks_ref, gkw_ref, cos_ref, sin_ref,
                 cmp_ref, sel_ref, win_ref, moba_ref, qn_ref, qr_ref, qb_ref, ga_ref, gm_ref, *copies):
    ht = _rms_rows(x_ref[0], gn_ref[...]).T.astype(BF16)
    cos, sin = cos_ref[...], sin_ref[...]
    kvw = NSA_KV_WIDTH
    mm = lambda r0, rows: jnp.dot(w_ref[r0:r0 + rows, :], ht, preferred_element_type=F32)
    c_rows_ref, k_rows_ref, v_tt_ref = copies if copies else (None, None, None)

    qn, qr = _norm_rope_heads(mm(_R_QA, NSA_WIDTH), gqa_ref, cos, sin)
    qn_ref[0] = qn.astype(BF16)
    qr_ref[0] = qr.astype(BF16)
    qb_ref[0] = _norm_rope_heads(mm(_R_QB, MOBA_WIDTH), gqb_ref, cos, sin)[1].astype(BF16)
    kb = _norm_rope_heads(mm(_R_KB, MOBA_WIDTH), gkb_ref, cos, sin)[1]
    vb = mm(_R_VB, MOBA_WIDTH)
    moba_ref[0, :MOBA_WIDTH, :] = kb
    moba_ref[0, MOBA_WIDTH:, :] = vb
    kv4 = mm(_R_KV4, 4 * kvw)
    ks = _norm_rope_heads(kv4[0:kvw], gks_ref, cos, sin)[1]
    kw = _norm_rope_heads(kv4[2 * kvw:3 * kvw], gkw_ref, cos, sin)[1]
    vs, vw = kv4[kvw:2 * kvw], kv4[3 * kvw:4 * kvw]
    sel_ref[0, :kvw, :] = ks
    sel_ref[0, kvw:, :] = vs
    win_ref[0, :kvw, :] = kw
    win_ref[0, kvw:, :] = vw
    c = mm(_R_C, 2 * kvw)
    cmp_ref[0] = c
    ga_ref[0] = _sigmoid(mm(_R_GA, HEAD_DIM))
    for k in range(2 * D_MODEL // GM_CHUNK):
        gm_ref[:, k * GM_CHUNK:(k + 1) * GM_CHUNK] = _sigmoid(mm(_R_GM + k * GM_CHUNK, GM_CHUNK)).T
    if copies:
        c_rows_ref[0] = c.T
        k_rows_ref[0, :, :MOBA_WIDTH] = kb.T.astype(BF16)
        k_rows_ref[0, :, MOBA_WIDTH:MOBA_WIDTH + kvw] = ks.T.astype(BF16)
        k_rows_ref[0, :, MOBA_WIDTH + kvw:] = kw.T.astype(BF16)
        v_tt_ref[0, 0, :MOBA_WIDTH, :] = vb.astype(BF16)
        v_tt_ref[0, 0, MOBA_WIDTH:MOBA_WIDTH + kvw, :] = vs.astype(BF16)
        v_tt_ref[0, 0, MOBA_WIDTH + kvw:, :] = vw.astype(BF16)


def _proj(x, gn, wt, gains, cos, sin, *, tm, attention_copies):
    b, t, d = x.shape
    kvw = NSA_KV_WIDTH
    n_pos_tiles = cos.shape[1] // tm
    tiles_per_kv = KV_TILE // tm if attention_copies else 1
    fixed2 = lambda bb, i: (0, 0)
    feat = lambda rows, dtype: (jax.ShapeDtypeStruct((b, rows, t), dtype),
                                pl.BlockSpec((1, rows, tm), lambda bb, i: (bb, 0, i)))
    outs = [feat(2 * kvw, F32), feat(2 * kvw, F32), feat(2 * kvw, F32), feat(2 * MOBA_WIDTH, F32),
            feat(NSA_WIDTH, BF16), feat(NSA_WIDTH, BF16), feat(MOBA_WIDTH, BF16), feat(HEAD_DIM, F32),
            (jax.ShapeDtypeStruct((b * t, 2 * d), F32),
             pl.BlockSpec((tm, 2 * d), lambda bb, i: (bb * (t // tm) + i, 0)))]
    if attention_copies:
        kcols = MOBA_WIDTH + 2 * kvw
        outs += [(jax.ShapeDtypeStruct((b, t, 2 * kvw), F32), pl.BlockSpec((1, tm, 2 * kvw), lambda bb, i: (bb, i, 0))),
                 (jax.ShapeDtypeStruct((b, t, kcols), BF16), pl.BlockSpec((1, tm, kcols), lambda bb, i: (bb, i, 0))),
                 (jax.ShapeDtypeStruct((b, t // KV_TILE, kcols, KV_TILE), BF16),
                  pl.BlockSpec((1, 1, kcols, tm), lambda bb, i: (bb, i // tiles_per_kv, 0, i % tiles_per_kv)))]
    res = pl.pallas_call(
        _proj_kernel,
        grid=(b, t // tm),
        in_specs=[pl.BlockSpec((1, tm, d), lambda bb, i: (bb, i, 0)),
                  pl.BlockSpec((1, d), fixed2),
                  pl.BlockSpec(wt.shape, fixed2)]
        + [pl.BlockSpec((HEAD_DIM, 1), fixed2)] * len(gains)
        + [pl.BlockSpec((HALF, tm), lambda bb, i: (0, i % n_pos_tiles))] * 2,
        out_specs=[o[1] for o in outs],
        out_shape=[o[0] for o in outs],
        compiler_params=_cparams(2),
        name="in_proj",
    )(x, gn.reshape(1, d), wt, *[g.reshape(HEAD_DIM, 1) for g in gains], cos, sin)
    names = ["cmp_t", "sel_t", "win_t", "moba_t", "qn_t", "qr_t", "qb_t", "ga_t", "gm", "c_rows", "k_rows", "v_tt"]
    return dict(zip(names, res))


def _gelu_tanh(x):
    return 0.5 * x * (1.0 + jnp.tanh(0.7978845608028654 * (x + 0.044715 * x * x * x)))


def _compress_blocks(load_rows, n_cmp, w1_ref, pos_ref, w2_ref, gk_ref):
    out = []
    for kv in range(2):
        sub = jnp.concatenate([load_rows(kv, j) for j in range(CMP_STRIDE)], axis=1).astype(BF16)
        hcat = jnp.dot(sub, w1_ref[kv], preferred_element_type=F32)
        bias = jnp.dot(pos_ref[kv], w1_ref[kv], preferred_element_type=F32)
        acc = None
        for g in range(NSA_KV_GROUPS):
            c0 = g * 2 * CMP_HIDDEN
            lo, hi = hcat[:, c0:c0 + CMP_HIDDEN], hcat[:, c0 + CMP_HIDDEN:c0 + 2 * CMP_HIDDEN]
            b = bias[0:1, c0:c0 + CMP_HIDDEN] + bias[1:2, c0 + CMP_HIDDEN:c0 + 2 * CMP_HIDDEN]
            hidden = lo + pltpu.roll(hi, n_cmp - 1, axis=0) + b
            part = jnp.dot(_gelu_tanh(hidden).astype(BF16), w2_ref[kv * NSA_KV_GROUPS + g],
                           preferred_element_type=F32)
            acc = part if acc is None else acc + part
        out.append(acc.T)
    parts = []
    for g in range(NSA_KV_GROUPS):
        blk = out[0][g * HEAD_DIM:(g + 1) * HEAD_DIM, :]
        ms = jnp.sum(blk * blk, axis=0, keepdims=True) * (1.0 / HEAD_DIM)
        parts.append(blk * lax.rsqrt(ms + NORM_EPS) * gk_ref[...])
    return jnp.concatenate(parts, axis=0), out[1]


def _compress_kernel(xk_ref, xv_ref, w1_ref, pos_ref, w2_ref, gk_ref, kc_ref, vct_ref):
    n_cmp = xk_ref.shape[1] // CMP_STRIDE
    x_refs = (xk_ref, xv_ref)
    kct, vct = _compress_blocks(lambda kv, j: x_refs[kv][0, pl.ds(j, n_cmp, stride=CMP_STRIDE), :],
                                n_cmp, w1_ref, pos_ref, w2_ref, gk_ref)
    kc_ref[0] = kct.T.astype(BF16)
    vct_ref[0] = vct.astype(BF16)


def _compress(c_rows, w1kv, poskv, w2pad, gk):
    nb, t, _ = c_rows.shape
    n_cmp = t // CMP_STRIDE
    full = lambda a: pl.BlockSpec(a.shape, lambda i: (0,) * a.ndim)
    return pl.pallas_call(
        _compress_kernel,
        grid=(nb,),
        in_specs=[pl.BlockSpec((1, t, NSA_KV_WIDTH), lambda i: (i, 0, 0)),
                  pl.BlockSpec((1, t, NSA_KV_WIDTH), lambda i: (i, 0, 1)),
                  full(w1kv), full(poskv), full(w2pad),
                  pl.BlockSpec((HEAD_DIM, 1), lambda i: (0, 0))],
        out_specs=[pl.BlockSpec((1, n_cmp, NSA_KV_WIDTH), lambda i: (i, 0, 0)),
                   pl.BlockSpec((1, NSA_KV_WIDTH, n_cmp), lambda i: (i, 0, 0))],
        out_shape=[jax.ShapeDtypeStruct((nb, n_cmp, NSA_KV_WIDTH), BF16),
                   jax.ShapeDtypeStruct((nb, NSA_KV_WIDTH, n_cmp), BF16)],
        compiler_params=_cparams(1),
        name="compress",
    )(c_rows, c_rows, w1kv, poskv, w2pad, gk.reshape(HEAD_DIM, 1))


M_FLOOR = -1e29


def _softmax_scores(s, m, acc, vt_aug):
    m_new = jnp.maximum(m, jnp.max(s, axis=0, keepdims=True))
    p = jnp.exp(s - m_new).astype(BF16)
    return m_new, jnp.exp(m - m_new) * acc + jnp.dot(vt_aug, p, preferred_element_type=F32)


def _rank_rows(score, blk):
    rank = jnp.zeros(score.shape, F32)
    for j in range(score.shape[0]):
        sj = score[j:j + 1, :]
        beats = (sj > score) | ((sj == score) & (blk > j))
        rank = rank + beats.astype(F32)
    return rank


def _rep_rows(row8, n_rows):
    return jnp.concatenate([row8] * (n_rows // 8), axis=0)


def _nsa_kernel(qn_ref, qr_ref, ga_ref, kc_ref, vct_ref, ks_ref, vst_ref, kw_ref, vwt_ref,
                o_ref, sel_ref, *, tq, tk, n_sel):
    qi = pl.program_id(1)
    q0 = qi * tq
    n = NSA_REP * tq
    n_cmp = kc_ref.shape[1]
    blocks_per_tile = tk // SEL_BLOCK
    lane = lax.broadcasted_iota(jnp.int32, (1, n), 1)
    t_lane = q0 + (lane & (tq - 1))
    t_q = q0 + lax.broadcasted_iota(jnp.int32, (1, tq), 1)
    cur = t_q >> 6
    blk = lax.broadcasted_iota(jnp.int32, (n_sel, 1), 0)
    cend = lax.broadcasted_iota(jnp.int32, (n_cmp, 1), 0) * CMP_STRIDE + (CMP_LEN - 1)
    krow = lax.broadcasted_iota(jnp.int32, (tk, 1), 0)
    ci = lax.broadcasted_iota(jnp.int32, (n_sel, n_cmp), 1)
    mi = lax.broadcasted_iota(jnp.int32, (n_sel, n_cmp), 0) * (SEL_BLOCK // CMP_STRIDE)
    amat = ((ci >= mi) & (ci <= mi + 3)).astype(F32) + ((ci >= mi - 1) & (ci <= mi + 2)).astype(F32)
    zeros_q = jnp.zeros((HEAD_DIM, n), BF16)
    ones_v = jnp.ones((HEAD_DIM, tk), BF16)

    def q_pad(ref, g):
        q = jnp.concatenate([ref[0, (g * NSA_REP + r) * HEAD_DIM:(g * NSA_REP + r + 1) * HEAD_DIM, :]
                             for r in range(NSA_REP)], axis=1) * SCALE
        return jnp.concatenate([q, zeros_q] if g == 0 else [zeros_q, q], axis=0)

    def gate_row(branch, g):
        return jnp.concatenate([ga_ref[0, branch * NSA_HEADS + g * NSA_REP + r:
                                       branch * NSA_HEADS + g * NSA_REP + r + 1, :]
                                for r in range(NSA_REP)], axis=1)

    groups = range(NSA_KV_GROUPS)
    g_rows = [slice(g * HEAD_DIM, (g + 1) * HEAD_DIM) for g in groups]
    qr = [q_pad(qr_ref, g) for g in groups]
    o_c = []
    for g in groups:
        rows = g_rows[g]
        qn = q_pad(qn_ref, g)
        sc = jnp.dot(kc_ref[0], qn, preferred_element_type=F32)
        mask_c = cend <= t_lane
        smc = jnp.where(mask_c, sc, NEG)
        e = jnp.where(mask_c, jnp.exp(smc - jnp.max(smc, axis=0, keepdims=True)), 0.0)
        den = jnp.sum(e, axis=0, keepdims=True)
        pc = e / jnp.where(den > 0.0, den, 1.0)
        o_c.append(jnp.dot(vct_ref[0], pc.astype(BF16), preferred_element_type=F32)[rows, :])
        imp_n = jnp.dot(amat, pc, preferred_element_type=F32, precision=lax.Precision.HIGHEST)
        imp = imp_n[:, 0:tq]
        for r in range(1, NSA_REP):
            imp = imp + imp_n[:, r * tq:(r + 1) * tq]
        forced = (blk == 0) | (blk >= cur - 1)
        score = jnp.where(blk > cur, NEG, jnp.where(forced, FORCED, imp))
        sel = (_rank_rows(score, blk) < SEL_TOPK) & (blk <= cur)
        sel_n = jnp.concatenate([jnp.where(sel, 0.0, NEG)] * NSA_REP, axis=1)
        for m_ in range(n_sel):
            sel_ref[g, m_] = jnp.broadcast_to(sel_n[m_:m_ + 1, :], (8, n))

    def tile(kv, carry, with_win, final):
        k0 = pl.multiple_of(kv * tk, tk)
        ks_t = ks_ref[0, pl.ds(k0, tk), :]
        raw = [jnp.dot(ks_t, qr[g], preferred_element_type=F32) for g in groups]
        if with_win:
            kw_t = kw_ref[0, pl.ds(k0, tk), :]
            raw += [jnp.dot(kw_t, qr[g], preferred_element_type=F32) for g in groups]
        out = []
        for g in groups:
            s = raw[g] + jnp.concatenate([_rep_rows(sel_ref[g, kv * blocks_per_tile + b_], SEL_BLOCK)
                                          for b_ in range(blocks_per_tile)], axis=0)
            if final:
                s = jnp.where((kv * tk + krow) <= t_lane, s, NEG)
            out.append(_softmax_scores(s, *carry[g], jnp.concatenate([vst_ref[0, kv, g_rows[g], :], ones_v], axis=0)))
        if not with_win:
            return tuple(out) + tuple(carry[NSA_KV_GROUPS:])
        rel = t_lane - (kv * tk + krow)
        keep = (rel >= 0) if final else (rel < WINDOW)
        for g in groups:
            s = jnp.where(keep, raw[NSA_KV_GROUPS + g], NEG)
            out.append(_softmax_scores(s, *carry[NSA_KV_GROUPS + g],
                                       jnp.concatenate([vwt_ref[0, kv, g_rows[g], :], ones_v], axis=0)))
        return tuple(out)

    init = tuple((jnp.full((1, n), M_FLOOR, F32), jnp.zeros((2 * HEAD_DIM, n), F32)) for _ in range(2 * NSA_KV_GROUPS))
    last = (q0 + tq - 1) // tk
    prev = jnp.maximum(last - 1, 0)
    carry = lax.fori_loop(0, prev, lambda kv, c: tile(kv, c, False, False), init)
    carry = lax.fori_loop(prev, last, lambda kv, c: tile(kv, c, True, False), carry)
    carry = tile(last, carry, True, True)

    for g in groups:
        acc_s, acc_w = carry[g][1], carry[NSA_KV_GROUPS + g][1]
        o_s = acc_s[:HEAD_DIM] / acc_s[HEAD_DIM:HEAD_DIM + 1]
        o_w = acc_w[:HEAD_DIM] / acc_w[HEAD_DIM:HEAD_DIM + 1]
        o = gate_row(0, g) * o_c[g] + gate_row(1, g) * o_s + gate_row(2, g) * o_w
        o_heads = jnp.concatenate([o[:, r * tq:(r + 1) * tq] for r in range(NSA_REP)], axis=0)
        o_ref[0, :, g * NSA_REP * HEAD_DIM:(g + 1) * NSA_REP * HEAD_DIM] = o_heads.T.astype(BF16)


def _nsa(qn_t, qr_t, ga_t, kc, vc_t, k_rows, ks_col, vs_tt, vs_row, kw_rows, kw_col, vw_tt, vw_row):
    b, _, tq_total = qn_t.shape
    tq, tk = NSA_Q_TILE, KV_TILE
    assert WINDOW == tk and tk % tq == 0 and tq_total % tk == 0
    n_cmp = kc.shape[1]
    l_sel = k_rows.shape[1]
    n_sel = ((l_sel // SEL_BLOCK) + 7) // 8 * 8
    l_win = kw_rows.shape[1]
    n = NSA_REP * tq
    return pl.pallas_call(
        functools.partial(_nsa_kernel, tq=tq, tk=tk, n_sel=n_sel),
        grid=(b, tq_total // tq),
        in_specs=[
            pl.BlockSpec((1, NSA_WIDTH, tq), lambda bb, i: (bb, 0, i)),
            pl.BlockSpec((1, NSA_WIDTH, tq), lambda bb, i: (bb, 0, i)),
            pl.BlockSpec((1, ga_t.shape[1], tq), lambda bb, i: (bb, 0, i)),
            pl.BlockSpec((1, n_cmp, NSA_KV_WIDTH), lambda bb, i: (bb, 0, 0)),
            pl.BlockSpec((1, NSA_KV_WIDTH, n_cmp), lambda bb, i: (bb, 0, 0)),
            pl.BlockSpec((1, l_sel, NSA_KV_WIDTH), lambda bb, i: (bb, 0, ks_col)),
            pl.BlockSpec((1, l_sel // tk, NSA_KV_WIDTH, tk), lambda bb, i: (bb, 0, vs_row, 0)),
            pl.BlockSpec((1, l_win, NSA_KV_WIDTH), lambda bb, i: (bb, 0, kw_col)),
            pl.BlockSpec((1, l_win // tk, NSA_KV_WIDTH, tk), lambda bb, i: (bb, 0, vw_row, 0)),
        ],
        out_specs=pl.BlockSpec((1, tq, NSA_WIDTH), lambda bb, i: (bb, i, 0)),
        out_shape=jax.ShapeDtypeStruct((b, tq_total, NSA_WIDTH), BF16),
        scratch_shapes=[pltpu.VMEM((NSA_KV_GROUPS, n_sel, 8, n), F32)],
        compiler_params=_cparams(2),
        name="nsa_attend",
    )(qn_t, qr_t, ga_t, kc, vc_t, k_rows, vs_tt, kw_rows, vw_tt)


MOBA_CHAINS = 4


def _moba_kernel(q_ref, k_ref, vt_ref, o_ref, km_ref, sel_ref, *, tq, tk, n_blk):
    qi = pl.program_id(2)
    q0 = qi * tq
    n = 2 * tq
    pair = 2 * HEAD_DIM
    cq = q0 // MOBA_BLOCK
    blocks_per_tile = tk // MOBA_BLOCK
    n_rows = km_ref.shape[1]

    @pl.when(qi == 0)
    def _():
        kall = k_ref[0].astype(F32)
        km = jnp.sum(kall.reshape(n_blk, MOBA_BLOCK, MOBA_CHAINS * pair), axis=1) * (1.0 / MOBA_BLOCK)
        if n_rows > n_blk:
            km = jnp.concatenate([km, jnp.zeros((n_rows - n_blk, MOBA_CHAINS * pair), F32)], axis=0)
        for c in range(MOBA_CHAINS):
            km_ref[c] = km[:, c * pair:(c + 1) * pair].astype(BF16)

    lane = lax.broadcasted_iota(jnp.int32, (1, n), 1)
    t_lane = q0 + (lane & (tq - 1))
    blk = lax.broadcasted_iota(jnp.int32, (n_rows, 1), 0)
    krow = lax.broadcasted_iota(jnp.int32, (MOBA_BLOCK, 1), 0)
    zq = jnp.zeros((HEAD_DIM, tq), BF16)
    ones_v = jnp.ones((16, tk), BF16)
    past = blk < cq
    qbds = []
    for c in range(MOBA_CHAINS):
        r0 = c * pair
        qbd = jnp.concatenate(
            [jnp.concatenate([q_ref[0, r0:r0 + HEAD_DIM, :] * SCALE, zq], axis=0),
             jnp.concatenate([zq, q_ref[0, r0 + HEAD_DIM:r0 + pair, :] * SCALE], axis=0)], axis=1)
        qbds.append(qbd)
        s_blk = jnp.dot(km_ref[c], qbd, preferred_element_type=F32) * (1.0 / SCALE)
        score = jnp.where(past, s_blk, NEG)
        sel_b = jnp.where((_rank_rows(score, blk) < MOBA_TOPK) & past, 0.0, NEG)
        for m_ in range(n_rows):
            sel_ref[c, m_] = jnp.broadcast_to(sel_b[m_:m_ + 1, :], (8, n))

    def tile(kv, carry, own_tile):
        raw = [jnp.dot(k_ref[0, pl.ds(pl.multiple_of(kv * tk, tk), tk), c * pair:(c + 1) * pair], qbds[c],
                       preferred_element_type=F32) for c in range(MOBA_CHAINS)]
        out = []
        for c in range(MOBA_CHAINS):
            m, acc = carry[c]
            parts = []
            for b_ in range(blocks_per_tile):
                nb = kv * blocks_per_tile + b_
                sb = raw[c][b_ * MOBA_BLOCK:(b_ + 1) * MOBA_BLOCK, :]
                picked = sb + _rep_rows(sel_ref[c, nb], MOBA_BLOCK)
                if own_tile:
                    own = jnp.where((nb * MOBA_BLOCK + krow) <= t_lane, sb, NEG)
                    picked = jnp.where(nb == cq, own, picked)
                parts.append(picked)
            vt_aug = jnp.concatenate([vt_ref[0, kv, c * pair:(c + 1) * pair, :], ones_v], axis=0)
            out.append(_softmax_scores(jnp.concatenate(parts, axis=0), m, acc, vt_aug))
        return tuple(out)

    init = tuple((jnp.full((1, n), M_FLOOR, F32), jnp.zeros((pair + 16, n), F32)) for _ in range(MOBA_CHAINS))
    last = (q0 + tq - 1) // tk
    carry = lax.fori_loop(0, last, lambda kv, cr: tile(kv, cr, False), init)
    carry = tile(last, carry, True)
    for c in range(MOBA_CHAINS):
        acc = carry[c][1]
        o = acc[:pair] / acc[pair:pair + 1]
        o2 = jnp.concatenate([o[0:HEAD_DIM, 0:tq], o[HEAD_DIM:pair, tq:2 * tq]], axis=0)
        o_ref[0, :, c * pair:(c + 1) * pair] = o2.T.astype(BF16)


def _moba(q_t, k_rows, k_col0, v_tt, v_row0):
    b, _, tq_total = q_t.shape
    tq, tk = MOBA_Q_TILE, KV_TILE
    width = MOBA_CHAINS * 2 * HEAD_DIM
    assert tq == MOBA_BLOCK and tk % tq == 0 and k_col0 % width == 0 and v_row0 % width == 0
    l_kv = k_rows.shape[1]
    n_blk = l_kv // MOBA_BLOCK
    n_rows = (n_blk + 7) // 8 * 8
    n = 2 * tq
    return pl.pallas_call(
        functools.partial(_moba_kernel, tq=tq, tk=tk, n_blk=n_blk),
        grid=(b, MOBA_WIDTH // width, tq_total // tq),
        in_specs=[
            pl.BlockSpec((1, width, tq), lambda bb, hp, i: (bb, hp, i)),
            pl.BlockSpec((1, l_kv, width), lambda bb, hp, i: (bb, 0, k_col0 // width + hp)),
            pl.BlockSpec((1, l_kv // tk, width, tk), lambda bb, hp, i: (bb, 0, v_row0 // width + hp, 0)),
        ],
        out_specs=pl.BlockSpec((1, tq, width), lambda bb, hp, i: (bb, i, hp)),
        out_shape=jax.ShapeDtypeStruct((b, tq_total, MOBA_WIDTH), BF16),
        scratch_shapes=[pltpu.VMEM((MOBA_CHAINS, n_rows, 2 * HEAD_DIM), BF16),
                        pltpu.VMEM((MOBA_CHAINS, n_rows, 8, n), F32)],
        compiler_params=_cparams(3),
        name="moba_attend",
    )(q_t, k_rows, v_tt)


def _merge_kernel(x_ref, oa_ref, ob_ref, gm_ref, wa_ref, wb_ref, wo_ref, o_ref):
    ua = jnp.dot(oa_ref[...], wa_ref[...], preferred_element_type=F32)
    ub = jnp.dot(ob_ref[...], wb_ref[...], preferred_element_type=F32)
    mixed = gm_ref[:, :D_MODEL] * ua + gm_ref[:, D_MODEL:] * ub
    o_ref[...] = x_ref[...] + jnp.dot(mixed.astype(BF16), wo_ref[...], preferred_element_type=F32)


def _merge(x2, oa, ob, gm, wa, wb, wo, *, tm):
    m, d = x2.shape
    row = lambda i: (i, 0)
    fixed = lambda i: (0, 0)
    return pl.pallas_call(
        _merge_kernel,
        grid=(m // tm,),
        in_specs=[pl.BlockSpec((tm, d), row), pl.BlockSpec((tm, NSA_WIDTH), row),
                  pl.BlockSpec((tm, MOBA_WIDTH), row), pl.BlockSpec((tm, 2 * d), row),
                  pl.BlockSpec(wa.shape, fixed), pl.BlockSpec(wb.shape, fixed), pl.BlockSpec(wo.shape, fixed)],
        out_specs=pl.BlockSpec((tm, d), row),
        out_shape=jax.ShapeDtypeStruct((m, d), F32),
        compiler_params=_cparams(1),
        name="merge_out",
    )(x2, oa, ob, gm, wa, wb, wo)


def _mlp_kernel(x_ref, g_ref, w1_ref, w2_ref, o_ref, h_ref, acc_ref):
    c = pl.program_id(1)

    @pl.when(c == 0)
    def _():
        h_ref[...] = _rms_rows(x_ref[...], g_ref[...]).astype(BF16)
        acc_ref[...] = x_ref[...]

    u = jnp.maximum(jnp.dot(h_ref[...], w1_ref[...].astype(BF16), preferred_element_type=F32), 0.0)
    acc_ref[...] += jnp.dot((u * u).astype(BF16), w2_ref[...].astype(BF16), preferred_element_type=F32)

    @pl.when(c == pl.num_programs(1) - 1)
    def _():
        o_ref[...] = acc_ref[...]


def _mlp(x2, g, w1_all, w2_all, layer, *, tm, tf):
    m, d = x2.shape
    dff = w1_all.shape[2]
    return pl.pallas_call(
        _mlp_kernel,
        grid=(m // tm, dff // tf),
        in_specs=[pl.BlockSpec((tm, d), lambda i, c: (i, 0)),
                  pl.BlockSpec((1, d), lambda i, c: (0, 0)),
                  pl.BlockSpec((None, d, tf), lambda i, c: (layer, 0, c)),
                  pl.BlockSpec((None, tf, d), lambda i, c: (layer, c, 0))],
        out_specs=pl.BlockSpec((tm, d), lambda i, c: (i, 0)),
        out_shape=jax.ShapeDtypeStruct((m, d), F32),
        scratch_shapes=[pltpu.VMEM((tm, d), BF16), pltpu.VMEM((tm, d), F32)],
        compiler_params=_cparams(2),
        name="sq_relu_mlp",
    )(x2, g.reshape(1, d), w1_all, w2_all)


LANES = 128


def _stack_rows(rows, n_rows=8):
    idx = lax.broadcasted_iota(jnp.int32, (n_rows, 1), 0)
    out = jnp.zeros((n_rows, rows[0].shape[1]), F32)
    for i, r in enumerate(rows):
        out = jnp.where(idx == i, r, out)
    return out


def _lane_col(x, j, lane):
    return jnp.sum(jnp.where(lane == j, x, 0.0), axis=1, keepdims=True)


def _heads_on_rows(q_row, n_heads, feats):
    rows8 = lax.broadcasted_iota(jnp.int32, (8, feats), 0)
    lanes = lax.broadcasted_iota(jnp.int32, (8, feats), 1)
    if feats == n_heads * HEAD_DIM:
        return jnp.where((lanes // HEAD_DIM) == rows8, jnp.broadcast_to(q_row, (8, feats)), 0.0)
    rows = []
    for h in range(n_heads):
        c, off, g = h // 2, (h % 2) * HEAD_DIM, h // NSA_REP
        r = q_row[:, c * LANES:(c + 1) * LANES]
        rows.append(r if off == g * HEAD_DIM else pltpu.roll(r, HEAD_DIM, axis=1))
    q8 = _stack_rows(rows)
    return jnp.where((lanes // HEAD_DIM) == (rows8 // NSA_REP), q8, 0.0)


def _attend_pages(q8, kts, vts, masks, k_new, v_new, new_ok):
    qb = q8.astype(BF16)
    s = [jnp.where(masks[p], jnp.dot(qb, kts[p].astype(BF16), preferred_element_type=F32), NEG)
         for p in range(len(kts))]
    s_new = jnp.where(new_ok, jnp.sum(q8 * k_new, axis=1, keepdims=True), NEG)
    m = s[0]
    for sp in s[1:]:
        m = jnp.maximum(m, sp)
    m = jnp.maximum(jnp.max(m, axis=1, keepdims=True), s_new)
    e_new = jnp.where(new_ok, jnp.exp(s_new - m), 0.0)
    den = e_new
    acc = e_new * v_new
    for p in range(len(kts)):
        e = jnp.where(masks[p], jnp.exp(s[p] - m), 0.0)
        den = den + jnp.sum(e, axis=1, keepdims=True)
        acc = acc + lax.dot_general(e.astype(BF16), vts[p].astype(BF16), (((1,), (1,)), ((), ())),
                                    preferred_element_type=F32)
    return acc / den


def _decode_kernel(pt_ref, *refs, n_pages, t_pos, w_buf):
    del pt_ref
    cmp_pages = refs[:n_pages]
    sel_pages = refs[n_pages:2 * n_pages]
    moba_pages = refs[2 * n_pages:3 * n_pages]
    (win_ref, q_ref, new_ref, ga_ref, w1_ref, pos_ref, w2_ref, gk_ref,
     oa_ref, ob_ref, xk_ref, xv_ref) = refs[3 * n_pages:]
    past = n_pages * PAGE_SIZE
    n_cmp = past // CMP_STRIDE
    lane = lax.broadcasted_iota(jnp.int32, (1, LANES), 1)
    sub_i = lax.broadcasted_iota(jnp.int32, (LANES, 1), 0)
    row8 = lax.broadcasted_iota(jnp.int32, (8, 1), 0)

    kvw = NSA_KV_WIDTH
    qn8 = _heads_on_rows(q_ref[0, :, 0:NSA_WIDTH], NSA_HEADS, kvw) * SCALE
    qr8 = _heads_on_rows(q_ref[0, :, NSA_WIDTH:2 * NSA_WIDTH], NSA_HEADS, kvw) * SCALE
    qb8 = _heads_on_rows(q_ref[0, :, 2 * NSA_WIDTH:2 * NSA_WIDTH + MOBA_WIDTH], MOBA_HEADS, MOBA_WIDTH)
    new = lambda off, width: new_ref[0, :, off:off + width]
    ks_new, vs_new, kw_new, vw_new = new(0, kvw), new(kvw, kvw), new(2 * kvw, kvw), new(3 * kvw, kvw)
    kb_new, vb_new = new(4 * kvw, MOBA_WIDTH), new(4 * kvw + MOBA_WIDTH, MOBA_WIDTH)
    ga_cols = jnp.broadcast_to(ga_ref[0], (LANES, LANES)).T
    gate = [ga_cols[br * NSA_HEADS:(br + 1) * NSA_HEADS, 0:1] for br in range(3)]
    grp8 = row8 // NSA_REP
    mpage = lambda p, kv: moba_pages[p][kv].reshape(MOBA_WIDTH, PAGE_SIZE).astype(BF16)
    qbb = qb8.astype(BF16)
    raw = [jnp.dot(qbb, mpage(p, 0), preferred_element_type=F32) for p in range(n_pages)]

    x_refs = (xk_ref, xv_ref)
    for p in range(n_pages):
        for kv in range(2):
            x_refs[kv][p * PAGE_SIZE:(p + 1) * PAGE_SIZE, :] = cmp_pages[p][kv].reshape(kvw, PAGE_SIZE).T
    ck_t = _compress_blocks(lambda kv, j: x_refs[kv][pl.ds(j, n_cmp, stride=CMP_STRIDE), :],
                            n_cmp, w1_ref, pos_ref, w2_ref, gk_ref)

    cend_ok = (lane * CMP_STRIDE + (CMP_LEN - 1)) <= t_pos
    cur = t_pos // SEL_BLOCK
    forced = (lane == 0) | (lane >= cur - 1)
    mi = lane * (SEL_BLOCK // CMP_STRIDE)
    amat_t = (((sub_i >= mi) & (sub_i <= mi + 3)).astype(F32)
              + ((sub_i >= mi - 1) & (sub_i <= mi + 2)).astype(F32))
    blocks_per_page = PAGE_SIZE // SEL_BLOCK

    kct, vct = ck_t
    sm = jnp.where(cend_ok, jnp.dot(qn8.astype(BF16), kct.astype(BF16), preferred_element_type=F32), NEG)
    e = jnp.where(cend_ok, jnp.exp(sm - jnp.max(sm, axis=1, keepdims=True)), 0.0)
    den = jnp.sum(e, axis=1, keepdims=True)
    pc = e / jnp.where(den > 0.0, den, 1.0)
    o_c = lax.dot_general(pc.astype(BF16), vct.astype(BF16), (((1,), (1,)), ((), ())), preferred_element_type=F32)
    sel_rows = []
    for g in range(NSA_KV_GROUPS):
        psum = jnp.broadcast_to(jnp.sum(jnp.where(grp8 == g, pc, 0.0), axis=0, keepdims=True), (8, LANES))
        imp = jnp.dot(psum, amat_t, preferred_element_type=F32, precision=lax.Precision.HIGHEST)[0:1, :]
        score = jnp.where(lane > cur, NEG, jnp.where(forced, FORCED, imp))
        rm = jnp.broadcast_to(score, (LANES, LANES))
        cm = rm.T
        rank = jnp.sum(((cm > rm) | ((cm == rm) & (sub_i < lane))).astype(F32), axis=0, keepdims=True)
        sel_rows.append(((rank < SEL_TOPK) & (lane <= cur)).astype(F32))
    sel8 = jnp.where(grp8 == 0, sel_rows[0], sel_rows[1])
    masks = []
    for p in range(n_pages):
        picks = [_lane_col(sel8, p * blocks_per_page + b_, lane) for b_ in range(blocks_per_page)]
        mrow = picks[-1]
        for b_ in range(blocks_per_page - 2, -1, -1):
            mrow = jnp.where(lane < (b_ + 1) * SEL_BLOCK, picks[b_], mrow)
        masks.append(mrow > 0.5)
    page2d = lambda ref, kv: ref[kv].reshape(kvw, ref.shape[-1])
    o_s = _attend_pages(qr8, [page2d(sel_pages[p], 0) for p in range(n_pages)],
                        [page2d(sel_pages[p], 1) for p in range(n_pages)], masks,
                        ks_new, vs_new, _lane_col(sel8, cur, lane) > 0.5)
    wmasks = []
    for c in range(w_buf // LANES):
        rel = t_pos - (past - w_buf + c * LANES + lane)
        wmasks.append((rel >= 0) & (rel < WINDOW))
    win_k, win_v = page2d(win_ref, 0), page2d(win_ref, 1)
    o_w = _attend_pages(qr8, [win_k[:, c * LANES:(c + 1) * LANES] for c in range(w_buf // LANES)],
                        [win_v[:, c * LANES:(c + 1) * LANES] for c in range(w_buf // LANES)], wmasks,
                        kw_new, vw_new, row8 >= 0)
    o8 = gate[0] * o_c + gate[1] * o_s + gate[2] * o_w
    for c in range(NSA_HEADS // 2):
        even, odd = o8[2 * c:2 * c + 1, :], o8[2 * c + 1:2 * c + 2, :]
        if (2 * c) // NSA_REP == 0:
            odd = pltpu.roll(odd, HEAD_DIM, axis=1)
        else:
            even = pltpu.roll(even, HEAD_DIM, axis=1)
        oa_ref[0, :, c * LANES:(c + 1) * LANES] = jnp.where(lane < HEAD_DIM, even, odd)

    pages_per_blk = MOBA_BLOCK // PAGE_SIZE
    cq = t_pos // MOBA_BLOCK
    s_blk = jnp.zeros((MOBA_HEADS, LANES), F32)
    for n_ in range(past // MOBA_BLOCK):
        tot = raw[n_ * pages_per_blk]
        for k_ in range(1, pages_per_blk):
            tot = tot + raw[n_ * pages_per_blk + k_]
        s_blk = jnp.where(lane == n_, jnp.sum(tot, axis=1, keepdims=True) * (1.0 / MOBA_BLOCK), s_blk)
    past_m = lane < cq
    score = jnp.where(past_m, s_blk, NEG)
    rank = jnp.zeros((MOBA_HEADS, LANES), F32)
    for j in range(cq):
        cj = _lane_col(score, j, lane)
        rank = rank + ((cj > score) | ((cj == score) & (lane > j))).astype(F32)
    sel = (past_m & (rank < MOBA_TOPK)).astype(F32)
    picked = [_lane_col(sel, n_, lane) > 0.5 for n_ in range(past // MOBA_BLOCK)]
    s_new = jnp.sum(qb8 * kb_new, axis=1, keepdims=True) * SCALE
    sc = [jnp.where(picked[p // pages_per_blk], raw[p] * SCALE, NEG) for p in range(n_pages)]
    m = sc[0]
    for sp in sc[1:]:
        m = jnp.maximum(m, sp)
    m = jnp.maximum(jnp.max(m, axis=1, keepdims=True), s_new)
    e_new = jnp.exp(s_new - m)
    den = e_new
    acc = e_new * vb_new
    for p in range(n_pages):
        e = jnp.where(picked[p // pages_per_blk], jnp.exp(sc[p] - m), 0.0)
        den = den + jnp.sum(e, axis=1, keepdims=True)
        acc = acc + lax.dot_general(e.astype(BF16), mpage(p, 1), (((1,), (1,)), ((), ())),
                                    preferred_element_type=F32)
    rows8w = lax.broadcasted_iota(jnp.int32, (MOBA_HEADS, MOBA_WIDTH), 0)
    lanes_w = lax.broadcasted_iota(jnp.int32, (MOBA_HEADS, MOBA_WIDTH), 1)
    ob_ref[0] = jnp.sum(jnp.where((lanes_w // HEAD_DIM) == rows8w, acc / den, 0.0), axis=0, keepdims=True)


def _decode(cmp_t, sel_t, moba_t, win_t, layer, page_table, q_rows, new_rows, ga_rows, w1kv, poskv, w2pad, gk):
    n_seq, n_pages = page_table.shape
    past = n_pages * PAGE_SIZE
    w_buf = win_t.shape[-1]
    assert past // CMP_STRIDE == LANES and past % MOBA_BLOCK == 0 and w_buf % LANES == 0

    def page_spec(h, p):
        return pl.BlockSpec((None, None, 2, h, HEAD_DIM, PAGE_SIZE), lambda s, pt: (layer, pt[s, p], 0, 0, 0, 0))

    row_spec = lambda a: pl.BlockSpec((1, 1, a.shape[2]), lambda s, pt: (s, 0, 0))
    full = lambda a: pl.BlockSpec(a.shape, lambda s, pt: (0,) * a.ndim)
    in_specs = ([page_spec(NSA_KV_GROUPS, p) for p in range(n_pages)]
                + [page_spec(NSA_KV_GROUPS, p) for p in range(n_pages)]
                + [page_spec(MOBA_HEADS, p) for p in range(n_pages)]
                + [pl.BlockSpec((None, None, 2, NSA_KV_GROUPS, HEAD_DIM, w_buf), lambda s, pt: (layer, s, 0, 0, 0, 0)),
                   row_spec(q_rows), row_spec(new_rows), row_spec(ga_rows),
                   full(w1kv), full(poskv), full(w2pad), pl.BlockSpec((HEAD_DIM, 1), lambda s, pt: (0, 0))])
    out_spec = pl.BlockSpec((1, 1, NSA_WIDTH), lambda s, pt: (s, 0, 0))
    return pl.pallas_call(
        functools.partial(_decode_kernel, n_pages=n_pages, t_pos=past, w_buf=w_buf),
        grid_spec=pltpu.PrefetchScalarGridSpec(
            num_scalar_prefetch=1, grid=(n_seq,), in_specs=in_specs, out_specs=[out_spec, out_spec],
            scratch_shapes=[pltpu.VMEM((past, NSA_KV_WIDTH), F32)] * 2),
        out_shape=[jax.ShapeDtypeStruct((n_seq, 1, NSA_WIDTH), F32)] * 2,
        compiler_params=_cparams(1),
        name="decode_mixers",
    )(page_table, *([cmp_t] * n_pages), *([sel_t] * n_pages), *([moba_t] * n_pages), win_t,
      q_rows, new_rows, ga_rows, w1kv, poskv, w2pad, gk.reshape(HEAD_DIM, 1))


_SPLITS = (NSA_WIDTH, NSA_KV_WIDTH, NSA_KV_WIDTH, NSA_KV_WIDTH, NSA_KV_WIDTH, NSA_KV_WIDTH, NSA_KV_WIDTH,
           3 * NSA_HEADS, MOBA_WIDTH, MOBA_WIDTH, MOBA_WIDTH, 2 * D_MODEL)
_NAMES = ("qa", "kc", "vc", "ks", "vs", "kw", "vw", "ga", "qb", "kb", "vb", "gm")


def _layer_params(l, w_in, gq_nsa, gk_sel, gk_win, gq_moba, gk_moba, cmp_pos_k, cmp_pos_v,
                  cmp_w1_k, cmp_w2_k, cmp_w1_v, cmp_w2_v):
    wt = w_in[l].T
    off, part = 0, {}
    for name, size in zip(_NAMES, _SPLITS):
        part[name] = wt[off:off + size]
        off += size
    ga_pad = jnp.zeros((HEAD_DIM - 3 * NSA_HEADS, D_MODEL), F32)
    order = ("qa", "qb", "kb", "vb", "ks", "vs", "kw", "vw", "kc", "vc", "ga")
    p = {
        "wt": jnp.concatenate([part[n_] for n_ in order] + [ga_pad, part["gm"]], axis=0).astype(BF16),
        "gains": (gq_nsa[l], gq_moba[l], gk_moba[l], gk_sel[l], gk_win[l]),
    }
    w1kv, poskv, w2pad = [], [], []
    for w1, w2, pos in ((cmp_w1_k[l], cmp_w2_k[l], cmp_pos_k[l]), (cmp_w1_v[l], cmp_w2_v[l], cmp_pos_v[l])):
        w1r = w1.reshape(2, CMP_STRIDE, HEAD_DIM, CMP_HIDDEN)
        lohi = jnp.concatenate([w1r[0], w1r[1]], axis=-1)
        z = jnp.zeros_like(lohi)
        w1kv.append(jnp.stack([jnp.concatenate([lohi, z], axis=-1), jnp.concatenate([z, lohi], axis=-1)], axis=1)
                    .reshape(CMP_STRIDE * NSA_KV_WIDTH, NSA_KV_GROUPS * 2 * CMP_HIDDEN))
        posr = jnp.tile(pos.reshape(2, CMP_STRIDE, 1, HEAD_DIM), (1, 1, NSA_KV_GROUPS, 1))
        poskv.append(jnp.pad(posr.reshape(2, CMP_STRIDE * NSA_KV_WIDTH), ((0, 6), (0, 0))))
        zw = jnp.zeros_like(w2)
        w2pad += [jnp.concatenate([w2, zw], axis=1), jnp.concatenate([zw, w2], axis=1)]
    p["w1kv"] = jnp.stack(w1kv).astype(BF16)
    p["poskv"] = jnp.stack(poskv).astype(BF16)
    p["w2pad"] = jnp.stack(w2pad).astype(BF16)
    return p


def _rope_tables(pos):
    inv = ROPE_THETA ** (-jnp.arange(HALF, dtype=F32) / HALF)
    ang = inv[:, None] * pos.astype(F32)[None, :]
    return jnp.cos(ang), jnp.sin(ang)


def _cache_leaf(feat_major, heads):
    b, _, t = feat_major.shape
    return feat_major.reshape(b, 2, heads, HEAD_DIM, t).transpose(0, 4, 1, 2, 3)


def kernel(x_prompt, x_sample, cache_nsa_cmp, cache_nsa_sel, cache_moba, state_nsa_win, page_table,
           norm_mix, w_in, gq_nsa, gk_cmp, gk_sel, gk_win, gq_moba, gk_moba,
           cmp_pos_k, cmp_pos_v, cmp_w1_k, cmp_w2_k, cmp_w1_v, cmp_w2_v,
           w_up_nsa, w_up_moba, w_out, norm_mlp, w_mlp_up, w_mlp_down):
    depth = w_in.shape[0]
    b, t, d = x_prompt.shape
    n_dec = x_sample.shape[0]
    past_len = page_table.shape[1] * PAGE_SIZE

    cos_p, sin_p = _rope_tables(jnp.arange(t, dtype=jnp.int32))
    cos_s, sin_s = _rope_tables(jnp.full((n_dec,), past_len, dtype=jnp.int32))
    to_t = lambda c: c.transpose(0, 1, 3, 4, 5, 2)
    cmp_t, sel_t, moba_t, win_t = to_t(cache_nsa_cmp), to_t(cache_nsa_sel), to_t(cache_moba), to_t(state_nsa_win)

    xp = x_prompt
    xs = x_sample.reshape(1, n_dec, d)
    leaves = [[] for _ in range(8)]
    tm_p, tm_proj, tm_mlp = 512, 256, 1024
    kvw = NSA_KV_WIDTH
    for l in range(depth):
        p = _layer_params(l, w_in, gq_nsa, gk_sel, gk_win, gq_moba, gk_moba, cmp_pos_k, cmp_pos_v,
                          cmp_w1_k, cmp_w2_k, cmp_w1_v, cmp_w2_v)
        wa, wb, wo = w_up_nsa[l].astype(BF16), w_up_moba[l].astype(BF16), w_out[l].astype(BF16)

        pr = _proj(xp, norm_mix[l], p["wt"], p["gains"], cos_p, sin_p, tm=tm_proj, attention_copies=True)
        kc, vc_t = _compress(pr["c_rows"], p["w1kv"], p["poskv"], p["w2pad"], gk_cmp[l])
        ks_blk, kw_blk = MOBA_WIDTH // kvw, MOBA_WIDTH // kvw + 1
        oa = _nsa(pr["qn_t"], pr["qr_t"], pr["ga_t"], kc, vc_t,
                  pr["k_rows"], ks_blk, pr["v_tt"], ks_blk, pr["k_rows"], kw_blk, pr["v_tt"], kw_blk)
        ob = _moba(pr["qb_t"], pr["k_rows"], 0, pr["v_tt"], 0)
        x2 = _merge(xp.reshape(b * t, d), oa.reshape(b * t, NSA_WIDTH), ob.reshape(b * t, MOBA_WIDTH),
                    pr["gm"], wa, wb, wo, tm=tm_p)
        xp = _mlp(x2, norm_mlp[l], w_mlp_up, w_mlp_down, l, tm=tm_mlp, tf=1024).reshape(b, t, d)
        leaves[0].append(_cache_leaf(pr["cmp_t"], NSA_KV_GROUPS))
        leaves[1].append(_cache_leaf(pr["sel_t"], NSA_KV_GROUPS))
        leaves[2].append(_cache_leaf(pr["moba_t"], MOBA_HEADS))
        leaves[3].append(_cache_leaf(pr["win_t"][:, :, t - min(WINDOW, t):], NSA_KV_GROUPS))

        sr = _proj(xs, norm_mix[l], p["wt"], p["gains"], cos_s, sin_s, tm=n_dec, attention_copies=False)
        seq_rows = lambda a: a[0].T.astype(F32)
        c_new, new_sel, new_win, new_moba = (seq_rows(sr[k_]) for k_ in ("cmp_t", "sel_t", "win_t", "moba_t"))
        q_rows = jnp.concatenate([seq_rows(sr["qn_t"]), seq_rows(sr["qr_t"]), seq_rows(sr["qb_t"])], axis=1)
        new_rows = jnp.concatenate([new_sel, new_win, new_moba], axis=1)
        ga_rows = jnp.pad(seq_rows(sr["ga_t"]), ((0, 0), (0, LANES - HEAD_DIM)))
        oa, ob = _decode(cmp_t, sel_t, moba_t, win_t, l, page_table, q_rows[:, None, :], new_rows[:, None, :],
                         ga_rows[:, None, :], p["w1kv"], p["poskv"], p["w2pad"], gk_cmp[l])
        x2 = _merge(xs.reshape(n_dec, d), oa[:, 0].astype(BF16), ob[:, 0].astype(BF16), sr["gm"], wa, wb, wo, tm=n_dec)
        xs = _mlp(x2, norm_mlp[l], w_mlp_up, w_mlp_down, l, tm=n_dec, tf=1024).reshape(1, n_dec, d)
        seq_leaf = lambda rows, heads: rows.reshape(n_dec, 1, 2, heads, HEAD_DIM)
        leaves[4].append(seq_leaf(c_new, NSA_KV_GROUPS))
        leaves[5].append(seq_leaf(new_sel, NSA_KV_GROUPS))
        leaves[6].append(seq_leaf(new_moba, MOBA_HEADS))
        win_all = jnp.concatenate([state_nsa_win[l], seq_leaf(new_win, NSA_KV_GROUPS)], axis=1)
        leaves[7].append(win_all[:, -min(WINDOW, past_len + 1):])
    return (xp, xs.reshape(n_dec, 1, d)) + tuple(jnp.stack(v) for v in leaves)
```

```python
import functools

import jax
import jax.numpy as jnp
from jax import lax
from jax.experimental import pallas as pl
from jax.experimental.pallas import tpu as pltpu

F32 = jnp.float32
BF16 = jnp.bfloat16

D_MODEL = 1024
HEAD_DIM = 64
HALF = HEAD_DIM // 2
NSA_HEADS = 8
NSA_KV_GROUPS = 2
NSA_REP = NSA_HEADS // NSA_KV_GROUPS
MOBA_HEADS = 8
NSA_WIDTH = NSA_HEADS * HEAD_DIM
NSA_KV_WIDTH = NSA_KV_GROUPS * HEAD_DIM
MOBA_WIDTH = MOBA_HEADS * HEAD_DIM
CMP_LEN = 32
CMP_STRIDE = 16
CMP_HIDDEN = 2 * HEAD_DIM
SEL_BLOCK = 64
SEL_TOPK = 16
WINDOW = 512
MOBA_BLOCK = 256
MOBA_TOPK = 3
D_FF = 4 * D_MODEL
PAGE_SIZE = 128
ROPE_THETA = 10000.0
NORM_EPS = 1e-6
NEG = -1e30
FORCED = 1e6
SCALE = HEAD_DIM ** -0.5

V7X_VMEM_BYTES = 64 * 1024 * 1024
VMEM_LIMIT = V7X_VMEM_BYTES - 8 * 1024 * 1024
KV_TILE = 512
NSA_Q_TILE = 512
MOBA_Q_TILE = MOBA_BLOCK


def _cparams(n_axes):
    return pltpu.CompilerParams(dimension_semantics=("arbitrary",) * n_axes,
                                vmem_limit_bytes=VMEM_LIMIT)


def _rms_rows(x, g):
    ms = jnp.mean(x * x, axis=-1, keepdims=True)
    return x * lax.rsqrt(ms + NORM_EPS) * g


def _sigmoid(x):
    return 1.0 / (1.0 + jnp.exp(-x))


_R_QA, _R_QB, _R_KB, _R_VB = 0, NSA_WIDTH, NSA_WIDTH + MOBA_WIDTH, NSA_WIDTH + 2 * MOBA_WIDTH
_R_KV4 = _R_VB + MOBA_WIDTH
_R_C = _R_KV4 + 4 * NSA_KV_WIDTH
_R_GA = _R_C + 2 * NSA_KV_WIDTH
_R_GM = _R_GA + HEAD_DIM
_R_END = _R_GM + 2 * D_MODEL
GM_CHUNK = 512


def _norm_rope_heads(acc, g_ref, cos, sin):
    nparts, fparts = [], []
    for hh in range(acc.shape[0] // HEAD_DIM):
        blk = acc[hh * HEAD_DIM:(hh + 1) * HEAD_DIM, :]
        ms = jnp.sum(blk * blk, axis=0, keepdims=True) * (1.0 / HEAD_DIM)
        y = blk * lax.rsqrt(ms + NORM_EPS) * g_ref[...]
        nparts.append(y)
        y1, y2 = y[:HALF, :], y[HALF:, :]
        fparts += [y1 * cos - y2 * sin, y2 * cos + y1 * sin]
    return jnp.concatenate(nparts, axis=0), jnp.concatenate(fparts, axis=0)


def _proj_kernel(x_ref, gn_ref, w_ref, gqa_ref, gqb_ref, gkb_ref, gks_ref, gkw_ref, cos_ref, sin_ref,
                 cmp_ref, sel_ref, win_ref, moba_ref, qn_ref, qr_ref, qb_ref, ga_ref, gm_ref, *copies):
    ht = _rms_rows(x_ref[0], gn_ref[...]).T.astype(BF16)
    cos, sin = cos_ref[...], sin_ref[...]
    kvw = NSA_KV_WIDTH
    mm = lambda r0, rows: jnp.dot(w_ref[r0:r0 + rows, :], ht, preferred_element_type=F32)
    c_rows_ref, k_rows_ref, v_tt_ref = copies if copies else (None, None, None)

    qn, qr = _norm_rope_heads(mm(_R_QA, NSA_WIDTH), gqa_ref, cos, sin)
    qn_ref[0] = qn.astype(BF16)
    qr_ref[0] = qr.astype(BF16)
    qb_ref[0] = _norm_rope_heads(mm(_R_QB, MOBA_WIDTH), gqb_ref, cos, sin)[1].astype(BF16)
    kb = _norm_rope_heads(mm(_R_KB, MOBA_WIDTH), gkb_ref, cos, sin)[1]
    vb = mm(_R_VB, MOBA_WIDTH)
    moba_ref[0, :MOBA_WIDTH, :] = kb
    moba_ref[0, MOBA_WIDTH:, :] = vb
    kv4 = mm(_R_KV4, 4 * kvw)
    ks = _norm_rope_heads(kv4[0:kvw], gks_ref, cos, sin)[1]
    kw = _norm_rope_heads(kv4[2 * kvw:3 * kvw], gkw_ref, cos, sin)[1]
    vs, vw = kv4[kvw:2 * kvw], kv4[3 * kvw:4 * kvw]
    sel_ref[0, :kvw, :] = ks
    sel_ref[0, kvw:, :] = vs
    win_ref[0, :kvw, :] = kw
    win_ref[0, kvw:, :] = vw
    c = mm(_R_C, 2 * kvw)
    cmp_ref[0] = c
    ga_ref[0] = _sigmoid(mm(_R_GA, HEAD_DIM))
    for k in range(2 * D_MODEL // GM_CHUNK):
        gm_ref[:, k * GM_CHUNK:(k + 1) * GM_CHUNK] = _sigmoid(mm(_R_GM + k * GM_CHUNK, GM_CHUNK)).T
    if copies:
        c_rows_ref[0] = c.T
        k_rows_ref[0, :, :MOBA_WIDTH] = kb.T.astype(BF16)
        k_rows_ref[0, :, MOBA_WIDTH:MOBA_WIDTH + kvw] = ks.T.astype(BF16)
        k_rows_ref[0, :, MOBA_WIDTH + kvw:] = kw.T.astype(BF16)
        v_tt_ref[0, 0, :MOBA_WIDTH, :] = vb.astype(BF16)
        v_tt_ref[0, 0, MOBA_WIDTH:MOBA_WIDTH + kvw, :] = vs.astype(BF16)
        v_tt_ref[0, 0, MOBA_WIDTH + kvw:, :] = vw.astype(BF16)


def _proj(x, gn, wt, gains, cos, sin, *, tm, attention_copies):
    b, t, d = x.shape
    kvw = NSA_KV_WIDTH
    n_pos_tiles = cos.shape[1] // tm
    tiles_per_kv = KV_TILE // tm if attention_copies else 1
    fixed2 = lambda bb, i: (0, 0)
    feat = lambda rows, dtype: (jax.ShapeDtypeStruct((b, rows, t), dtype),
                                pl.BlockSpec((1, rows, tm), lambda bb, i: (bb, 0, i)))
    outs = [feat(2 * kvw, F32), feat(2 * kvw, F32), feat(2 * kvw, F32), feat(2 * MOBA_WIDTH, F32),
            feat(NSA_WIDTH, BF16), feat(NSA_WIDTH, BF16), feat(MOBA_WIDTH, BF16), feat(HEAD_DIM, F32),
            (jax.ShapeDtypeStruct((b * t, 2 * d), F32),
             pl.BlockSpec((tm, 2 * d), lambda bb, i: (bb * (t // tm) + i, 0)))]
    if attention_copies:
        kcols = MOBA_WIDTH + 2 * kvw
        outs += [(jax.ShapeDtypeStruct((b, t, 2 * kvw), F32), pl.BlockSpec((1, tm, 2 * kvw), lambda bb, i: (bb, i, 0))),
                 (jax.ShapeDtypeStruct((b, t, kcols), BF16), pl.BlockSpec((1, tm, kcols), lambda bb, i: (bb, i, 0))),
                 (jax.ShapeDtypeStruct((b, t // KV_TILE, kcols, KV_TILE), BF16),
                  pl.BlockSpec((1, 1, kcols, tm), lambda bb, i: (bb, i // tiles_per_kv, 0, i % tiles_per_kv)))]
    res = pl.pallas_call(
        _proj_kernel,
        grid=(b, t // tm),
        in_specs=[pl.BlockSpec((1, tm, d), lambda bb, i: (bb, i, 0)),
                  pl.BlockSpec((1, d), fixed2),
                  pl.BlockSpec(wt.shape, fixed2)]
        + [pl.BlockSpec((HEAD_DIM, 1), fixed2)] * len(gains)
        + [pl.BlockSpec((HALF, tm), lambda bb, i: (0, i % n_pos_tiles))] * 2,
        out_specs=[o[1] for o in outs],
        out_shape=[o[0] for o in outs],
        compiler_params=_cparams(2),
        name="in_proj",
    )(x, gn.reshape(1, d), wt, *[g.reshape(HEAD_DIM, 1) for g in gains], cos, sin)
    names = ["cmp_t", "sel_t", "win_t", "moba_t", "qn_t", "qr_t", "qb_t", "ga_t", "gm", "c_rows", "k_rows", "v_tt"]
    return dict(zip(names, res))


def _gelu_tanh(x):
    return 0.5 * x * (1.0 + jnp.tanh(0.7978845608028654 * (x + 0.044715 * x * x * x)))


def _compress_blocks(load_rows, n_cmp, w1_ref, pos_ref, w2_ref, gk_ref):
    out = []
    for kv in range(2):
        sub = jnp.concatenate([load_rows(kv, j) for j in range(CMP_STRIDE)], axis=1).astype(BF16)
        hcat = jnp.dot(sub, w1_ref[kv], preferred_element_type=F32)
        bias = jnp.dot(pos_ref[kv], w1_ref[kv], preferred_element_type=F32)
        acc = None
        for g in range(NSA_KV_GROUPS):
            c0 = g * 2 * CMP_HIDDEN
            lo, hi = hcat[:, c0:c0 + CMP_HIDDEN], hcat[:, c0 + CMP_HIDDEN:c0 + 2 * CMP_HIDDEN]
            b = bias[0:1, c0:c0 + CMP_HIDDEN] + bias[1:2, c0 + CMP_HIDDEN:c0 + 2 * CMP_HIDDEN]
            hidden = lo + pltpu.roll(hi, n_cmp - 1, axis=0) + b
            part = jnp.dot(_gelu_tanh(hidden).astype(BF16), w2_ref[kv * NSA_KV_GROUPS + g],
                           preferred_element_type=F32)
            acc = part if acc is None else acc + part
        out.append(acc.T)
    parts = []
    for g in range(NSA_KV_GROUPS):
        blk = out[0][g * HEAD_DIM:(g + 1) * HEAD_DIM, :]
        ms = jnp.sum(blk * blk, axis=0, keepdims=True) * (1.0 / HEAD_DIM)
        parts.append(blk * lax.rsqrt(ms + NORM_EPS) * gk_ref[...])
    return jnp.concatenate(parts, axis=0), out[1]


def _compress_kernel(xk_ref, xv_ref, w1_ref, pos_ref, w2_ref, gk_ref, kc_ref, vct_ref):
    n_cmp = xk_ref.shape[1] // CMP_STRIDE
    x_refs = (xk_ref, xv_ref)
    kct, vct = _compress_blocks(lambda kv, j: x_refs[kv][0, pl.ds(j, n_cmp, stride=CMP_STRIDE), :],
                                n_cmp, w1_ref, pos_ref, w2_ref, gk_ref)
    kc_ref[0] = kct.T.astype(BF16)
    vct_ref[0] = vct.astype(BF16)


def _compress(c_rows, w1kv, poskv, w2pad, gk):
    nb, t, _ = c_rows.shape
    n_cmp = t // CMP_STRIDE
    full = lambda a: pl.BlockSpec(a.shape, lambda i: (0,) * a.ndim)
    return pl.pallas_call(
        _compress_kernel,
        grid=(nb,),
        in_specs=[pl.BlockSpec((1, t, NSA_KV_WIDTH), lambda i: (i, 0, 0)),
                  pl.BlockSpec((1, t, NSA_KV_WIDTH), lambda i: (i, 0, 1)),
                  full(w1kv), full(poskv), full(w2pad),
                  pl.BlockSpec((HEAD_DIM, 1), lambda i: (0, 0))],
        out_specs=[pl.BlockSpec((1, n_cmp, NSA_KV_WIDTH), lambda i: (i, 0, 0)),
                   pl.BlockSpec((1, NSA_KV_WIDTH, n_cmp), lambda i: (i, 0, 0))],
        out_shape=[jax.ShapeDtypeStruct((nb, n_cmp, NSA_KV_WIDTH), BF16),
                   jax.ShapeDtypeStruct((nb, NSA_KV_WIDTH, n_cmp), BF16)],
        compiler_params=_cparams(1),
        name="compress",
    )(c_rows, c_rows, w1kv, poskv, w2pad, gk.reshape(HEAD_DIM, 1))


M_FLOOR = -1e29


def _softmax_scores(s, m, acc, vt_aug):
    m_new = jnp.maximum(m, jnp.max(s, axis=0, keepdims=True))
    p = jnp.exp(s - m_new).astype(BF16)
    return m_new, jnp.exp(m - m_new) * acc + jnp.dot(vt_aug, p, preferred_element_type=F32)


def _rank_rows(score, blk):
    rank = jnp.zeros(score.shape, F32)
    for j in range(score.shape[0]):
        sj = score[j:j + 1, :]
        beats = (sj > score) | ((sj == score) & (blk > j))
        rank = rank + beats.astype(F32)
    return rank


def _rep_rows(row8, n_rows):
    return jnp.concatenate([row8] * (n_rows // 8), axis=0)


def _nsa_kernel(qn_ref, qr_ref, ga_ref, kc_ref, vct_ref, ks_ref, vst_ref, kw_ref, vwt_ref,
                o_ref, sel_ref, *, tq, tk, n_sel):
    qi = pl.program_id(1)
    q0 = qi * tq
    n = NSA_REP * tq
    n_cmp = kc_ref.shape[1]
    blocks_per_tile = tk // SEL_BLOCK
    lane = lax.broadcasted_iota(jnp.int32, (1, n), 1)
    t_lane = q0 + (lane & (tq - 1))
    t_q = q0 + lax.broadcasted_iota(jnp.int32, (1, tq), 1)
    cur = t_q >> 6
    blk = lax.broadcasted_iota(jnp.int32, (n_sel, 1), 0)
    cend = lax.broadcasted_iota(jnp.int32, (n_cmp, 1), 0) * CMP_STRIDE + (CMP_LEN - 1)
    krow = lax.broadcasted_iota(jnp.int32, (tk, 1), 0)
    ci = lax.broadcasted_iota(jnp.int32, (n_sel, n_cmp), 1)
    mi = lax.broadcasted_iota(jnp.int32, (n_sel, n_cmp), 0) * (SEL_BLOCK // CMP_STRIDE)
    amat = ((ci >= mi) & (ci <= mi + 3)).astype(F32) + ((ci >= mi - 1) & (ci <= mi + 2)).astype(F32)
    zeros_q = jnp.zeros((HEAD_DIM, n), BF16)
    ones_v = jnp.ones((HEAD_DIM, tk), BF16)

    def q_pad(ref, g):
        q = jnp.concatenate([ref[0, (g * NSA_REP + r) * HEAD_DIM:(g * NSA_REP + r + 1) * HEAD_DIM, :]
                             for r in range(NSA_REP)], axis=1) * SCALE
        return jnp.concatenate([q, zeros_q] if g == 0 else [zeros_q, q], axis=0)

    def gate_row(branch, g):
        return jnp.concatenate([ga_ref[0, branch * NSA_HEADS + g * NSA_REP + r:
                                       branch * NSA_HEADS + g * NSA_REP + r + 1, :]
                                for r in range(NSA_REP)], axis=1)

    groups = range(NSA_KV_GROUPS)
    g_rows = [slice(g * HEAD_DIM, (g + 1) * HEAD_DIM) for g in groups]
    qr = [q_pad(qr_ref, g) for g in groups]
    o_c = []
    for g in groups:
        rows = g_rows[g]
        qn = q_pad(qn_ref, g)
        sc = jnp.dot(kc_ref[0], qn, preferred_element_type=F32)
        mask_c = cend <= t_lane
        smc = jnp.where(mask_c, sc, NEG)
        e = jnp.where(mask_c, jnp.exp(smc - jnp.max(smc, axis=0, keepdims=True)), 0.0)
        den = jnp.sum(e, axis=0, keepdims=True)
        pc = e / jnp.where(den > 0.0, den, 1.0)
        o_c.append(jnp.dot(vct_ref[0], pc.astype(BF16), preferred_element_type=F32)[rows, :])
        imp_n = jnp.dot(amat, pc, preferred_element_type=F32, precision=lax.Precision.HIGHEST)
        imp = imp_n[:, 0:tq]
        for r in range(1, NSA_REP):
            imp = imp + imp_n[:, r * tq:(r + 1) * tq]
        forced = (blk == 0) | (blk >= cur - 1)
        score = jnp.where(blk > cur, NEG, jnp.where(forced, FORCED, imp))
        sel = (_rank_rows(score, blk) < SEL_TOPK) & (blk <= cur)
        sel_n = jnp.concatenate([jnp.where(sel, 0.0, NEG)] * NSA_REP, axis=1)
        for m_ in range(n_sel):
            sel_ref[g, m_] = jnp.broadcast_to(sel_n[m_:m_ + 1, :], (8, n))

    def tile(kv, carry, with_win, final):
        k0 = pl.multiple_of(kv * tk, tk)
        ks_t = ks_ref[0, pl.ds(k0, tk), :]
        raw = [jnp.dot(ks_t, qr[g], preferred_element_type=F32) for g in groups]
        if with_win:
            kw_t = kw_ref[0, pl.ds(k0, tk), :]
            raw += [jnp.dot(kw_t, qr[g], preferred_element_type=F32) for g in groups]
        out = []
        for g in groups:
            s = raw[g] + jnp.concatenate([_rep_rows(sel_ref[g, kv * blocks_per_tile + b_], SEL_BLOCK)
                                          for b_ in range(blocks_per_tile)], axis=0)
            if final:
                s = jnp.where((kv * tk + krow) <= t_lane, s, NEG)
            out.append(_softmax_scores(s, *carry[g], jnp.concatenate([vst_ref[0, kv, g_rows[g], :], ones_v], axis=0)))
        if not with_win:
            return tuple(out) + tuple(carry[NSA_KV_GROUPS:])
        rel = t_lane - (kv * tk + krow)
        keep = (rel >= 0) if final else (rel < WINDOW)
        for g in groups:
            s = jnp.where(keep, raw[NSA_KV_GROUPS + g], NEG)
            out.append(_softmax_scores(s, *carry[NSA_KV_GROUPS + g],
                                       jnp.concatenate([vwt_ref[0, kv, g_rows[g], :], ones_v], axis=0)))
        return tuple(out)

    init = tuple((jnp.full((1, n), M_FLOOR, F32), jnp.zeros((2 * HEAD_DIM, n), F32)) for _ in range(2 * NSA_KV_GROUPS))
    last = (q0 + tq - 1) // tk
    prev = jnp.maximum(last - 1, 0)
    carry = lax.fori_loop(0, prev, lambda kv, c: tile(kv, c, False, False), init)
    carry = lax.fori_loop(prev, last, lambda kv, c: tile(kv, c, True, False), carry)
    carry = tile(last, carry, True, True)

    for g in groups:
        acc_s, acc_w = carry[g][1], carry[NSA_KV_GROUPS + g][1]
        o_s = acc_s[:HEAD_DIM] / acc_s[HEAD_DIM:HEAD_DIM + 1]
        o_w = acc_w[:HEAD_DIM] / acc_w[HEAD_DIM:HEAD_DIM + 1]
        o = gate_row(0, g) * o_c[g] + gate_row(1, g) * o_s + gate_row(2, g) * o_w
        o_heads = jnp.concatenate([o[:, r * tq:(r + 1) * tq] for r in range(NSA_REP)], axis=0)
        o_ref[0, :, g * NSA_REP * HEAD_DIM:(g + 1) * NSA_REP * HEAD_DIM] = o_heads.T.astype(BF16)


def _nsa(qn_t, qr_t, ga_t, kc, vc_t, k_rows, ks_col, vs_tt, vs_row, kw_rows, kw_col, vw_tt, vw_row):
    b, _, tq_total = qn_t.shape
    tq, tk = NSA_Q_TILE, KV_TILE
    assert WINDOW == tk and tk % tq == 0 and tq_total % tk == 0
    n_cmp = kc.shape[1]
    l_sel = k_rows.shape[1]
    n_sel = ((l_sel // SEL_BLOCK) + 7) // 8 * 8
    l_win = kw_rows.shape[1]
    n = NSA_REP * tq
    return pl.pallas_call(
        functools.partial(_nsa_kernel, tq=tq, tk=tk, n_sel=n_sel),
        grid=(b, tq_total // tq),
        in_specs=[
            pl.BlockSpec((1, NSA_WIDTH, tq), lambda bb, i: (bb, 0, i)),
            pl.BlockSpec((1, NSA_WIDTH, tq), lambda bb, i: (bb, 0, i)),
            pl.BlockSpec((1, ga_t.shape[1], tq), lambda bb, i: (bb, 0, i)),
            pl.BlockSpec((1, n_cmp, NSA_KV_WIDTH), lambda bb, i: (bb, 0, 0)),
            pl.BlockSpec((1, NSA_KV_WIDTH, n_cmp), lambda bb, i: (bb, 0, 0)),
            pl.BlockSpec((1, l_sel, NSA_KV_WIDTH), lambda bb, i: (bb, 0, ks_col)),
            pl.BlockSpec((1, l_sel // tk, NSA_KV_WIDTH, tk), lambda bb, i: (bb, 0, vs_row, 0)),
            pl.BlockSpec((1, l_win, NSA_KV_WIDTH), lambda bb, i: (bb, 0, kw_col)),
            pl.BlockSpec((1, l_win // tk, NSA_KV_WIDTH, tk), lambda bb, i: (bb, 0, vw_row, 0)),
        ],
        out_specs=pl.BlockSpec((1, tq, NSA_WIDTH), lambda bb, i: (bb, i, 0)),
        out_shape=jax.ShapeDtypeStruct((b, tq_total, NSA_WIDTH), BF16),
        scratch_shapes=[pltpu.VMEM((NSA_KV_GROUPS, n_sel, 8, n), F32)],
        compiler_params=_cparams(2),
        name="nsa_attend",
    )(qn_t, qr_t, ga_t, kc, vc_t, k_rows, vs_tt, kw_rows, vw_tt)


MOBA_CHAINS = 4


def _moba_kernel(q_ref, k_ref, vt_ref, o_ref, km_ref, sel_ref, *, tq, tk, n_blk):
    qi = pl.program_id(2)
    q0 = qi * tq
    n = 2 * tq
    pair = 2 * HEAD_DIM
    cq = q0 // MOBA_BLOCK
    blocks_per_tile = tk // MOBA_BLOCK
    n_rows = km_ref.shape[1]

    @pl.when(qi == 0)
    def _():
        kall = k_ref[0].astype(F32)
        km = jnp.sum(kall.reshape(n_blk, MOBA_BLOCK, MOBA_CHAINS * pair), axis=1) * (1.0 / MOBA_BLOCK)
        if n_rows > n_blk:
            km = jnp.concatenate([km, jnp.zeros((n_rows - n_blk, MOBA_CHAINS * pair), F32)], axis=0)
        for c in range(MOBA_CHAINS):
            km_ref[c] = km[:, c * pair:(c + 1) * pair].astype(BF16)

    lane = lax.broadcasted_iota(jnp.int32, (1, n), 1)
    t_lane = q0 + (lane & (tq - 1))
    blk = lax.broadcasted_iota(jnp.int32, (n_rows, 1), 0)
    krow = lax.broadcasted_iota(jnp.int32, (MOBA_BLOCK, 1), 0)
    zq = jnp.zeros((HEAD_DIM, tq), BF16)
    ones_v = jnp.ones((16, tk), BF16)
    past = blk < cq
    qbds = []
    for c in range(MOBA_CHAINS):
        r0 = c * pair
        qbd = jnp.concatenate(
            [jnp.concatenate([q_ref[0, r0:r0 + HEAD_DIM, :] * SCALE, zq], axis=0),
             jnp.concatenate([zq, q_ref[0, r0 + HEAD_DIM:r0 + pair, :] * SCALE], axis=0)], axis=1)
        qbds.append(qbd)
        s_blk = jnp.dot(km_ref[c], qbd, preferred_element_type=F32) * (1.0 / SCALE)
        score = jnp.where(past, s_blk, NEG)
        sel_b = jnp.where((_rank_rows(score, blk) < MOBA_TOPK) & past, 0.0, NEG)
        for m_ in range(n_rows):
            sel_ref[c, m_] = jnp.broadcast_to(sel_b[m_:m_ + 1, :], (8, n))

    def tile(kv, carry, own_tile):
        raw = [jnp.dot(k_ref[0, pl.ds(pl.multiple_of(kv * tk, tk), tk), c * pair:(c + 1) * pair], qbds[c],
                       preferred_element_type=F32) for c in range(MOBA_CHAINS)]
        out = []
        for c in range(MOBA_CHAINS):
            m, acc = carry[c]
            parts = []
            for b_ in range(blocks_per_tile):
                nb = kv * blocks_per_tile + b_
                sb = raw[c][b_ * MOBA_BLOCK:(b_ + 1) * MOBA_BLOCK, :]
                picked = sb + _rep_rows(sel_ref[c, nb], MOBA_BLOCK)
                if own_tile:
                    own = jnp.where((nb * MOBA_BLOCK + krow) <= t_lane, sb, NEG)
                    picked = jnp.where(nb == cq, own, picked)
                parts.append(picked)
            vt_aug = jnp.concatenate([vt_ref[0, kv, c * pair:(c + 1) * pair, :], ones_v], axis=0)
            out.append(_softmax_scores(jnp.concatenate(parts, axis=0), m, acc, vt_aug))
        return tuple(out)

    init = tuple((jnp.full((1, n), M_FLOOR, F32), jnp.zeros((pair + 16, n), F32)) for _ in range(MOBA_CHAINS))
    last = (q0 + tq - 1) // tk
    carry = lax.fori_loop(0, last, lambda kv, cr: tile(kv, cr, False), init)
    carry = tile(last, carry, True)
    for c in range(MOBA_CHAINS):
        acc = carry[c][1]
        o = acc[:pair] / acc[pair:pair + 1]
        o2 = jnp.concatenate([o[0:HEAD_DIM, 0:tq], o[HEAD_DIM:pair, tq:2 * tq]], axis=0)
        o_ref[0, :, c * pair:(c + 1) * pair] = o2.T.astype(BF16)


def _moba(q_t, k_rows, k_col0, v_tt, v_row0):
    b, _, tq_total = q_t.shape
    tq, tk = MOBA_Q_TILE, KV_TILE
    width = MOBA_CHAINS * 2 * HEAD_DIM
    assert tq == MOBA_BLOCK and tk % tq == 0 and k_col0 % width == 0 and v_row0 % width == 0
    l_kv = k_rows.shape[1]
    n_blk = l_kv // MOBA_BLOCK
    n_rows = (n_blk + 7) // 8 * 8
    n = 2 * tq
    return pl.pallas_call(
        functools.partial(_moba_kernel, tq=tq, tk=tk, n_blk=n_blk),
        grid=(b, MOBA_WIDTH // width, tq_total // tq),
        in_specs=[
            pl.BlockSpec((1, width, tq), lambda bb, hp, i: (bb, hp, i)),
            pl.BlockSpec((1, l_kv, width), lambda bb, hp, i: (bb, 0, k_col0 // width + hp)),
            pl.BlockSpec((1, l_kv // tk, width, tk), lambda bb, hp, i: (bb, 0, v_row0 // width + hp, 0)),
        ],
        out_specs=pl.BlockSpec((1, tq, width), lambda bb, hp, i: (bb, i, hp)),
        out_shape=jax.ShapeDtypeStruct((b, tq_total, MOBA_WIDTH), BF16),
        scratch_shapes=[pltpu.VMEM((MOBA_CHAINS, n_rows, 2 * HEAD_DIM), BF16),
                        pltpu.VMEM((MOBA_CHAINS, n_rows, 8, n), F32)],
        compiler_params=_cparams(3),
        name="moba_attend",
    )(q_t, k_rows, v_tt)


def _merge_kernel(x_ref, oa_ref, ob_ref, gm_ref, wa_ref, wb_ref, wo_ref, o_ref):
    ua = jnp.dot(oa_ref[...], wa_ref[...], preferred_element_type=F32)
    ub = jnp.dot(ob_ref[...], wb_ref[...], preferred_element_type=F32)
    mixed = gm_ref[:, :D_MODEL] * ua + gm_ref[:, D_MODEL:] * ub
    o_ref[...] = x_ref[...] + jnp.dot(mixed.astype(BF16), wo_ref[...], preferred_element_type=F32)


def _merge(x2, oa, ob, gm, wa, wb, wo, *, tm):
    m, d = x2.shape
    row = lambda i: (i, 0)
    fixed = lambda i: (0, 0)
    return pl.pallas_call(
        _merge_kernel,
        grid=(m // tm,),
        in_specs=[pl.BlockSpec((tm, d), row), pl.BlockSpec((tm, NSA_WIDTH), row),
                  pl.BlockSpec((tm, MOBA_WIDTH), row), pl.BlockSpec((tm, 2 * d), row),
                  pl.BlockSpec(wa.shape, fixed), pl.BlockSpec(wb.shape, fixed), pl.BlockSpec(wo.shape, fixed)],
        out_specs=pl.BlockSpec((tm, d), row),
        out_shape=jax.ShapeDtypeStruct((m, d), F32),
        compiler_params=_cparams(1),
        name="merge_out",
    )(x2, oa, ob, gm, wa, wb, wo)


def _mlp_kernel(x_ref, g_ref, w1_ref, w2_ref, o_ref, h_ref, acc_ref):
    c = pl.program_id(1)

    @pl.when(c == 0)
    def _():
        h_ref[...] = _rms_rows(x_ref[...], g_ref[...]).astype(BF16)
        acc_ref[...] = x_ref[...]

    u = jnp.maximum(jnp.dot(h_ref[...], w1_ref[...].astype(BF16), preferred_element_type=F32), 0.0)
    acc_ref[...] += jnp.dot((u * u).astype(BF16), w2_ref[...].astype(BF16), preferred_element_type=F32)

    @pl.when(c == pl.num_programs(1) - 1)
    def _():
        o_ref[...] = acc_ref[...]


def _mlp(x2, g, w1_all, w2_all, layer, *, tm, tf):
    m, d = x2.shape
    dff = w1_all.shape[2]
    return pl.pallas_call(
        _mlp_kernel,
        grid=(m // tm, dff // tf),
        in_specs=[pl.BlockSpec((tm, d), lambda i, c: (i, 0)),
                  pl.BlockSpec((1, d), lambda i, c: (0, 0)),
                  pl.BlockSpec((None, d, tf), lambda i, c: (layer, 0, c)),
                  pl.BlockSpec((None, tf, d), lambda i, c: (layer, c, 0))],
        out_specs=pl.BlockSpec((tm, d), lambda i, c: (i, 0)),
        out_shape=jax.ShapeDtypeStruct((m, d), F32),
        scratch_shapes=[pltpu.VMEM((tm, d), BF16), pltpu.VMEM((tm, d), F32)],
        compiler_params=_cparams(2),
        name="sq_relu_mlp",
    )(x2, g.reshape(1, d), w1_all, w2_all)


LANES = 128


def _stack_rows(rows, n_rows=8):
    idx = lax.broadcasted_iota(jnp.int32, (n_rows, 1), 0)
    out = jnp.zeros((n_rows, rows[0].shape[1]), F32)
    for i, r in enumerate(rows):
        out = jnp.where(idx == i, r, out)
    return out


def _lane_col(x, j, lane):
    return jnp.sum(jnp.where(lane == j, x, 0.0), axis=1, keepdims=True)


def _heads_on_rows(q_row, n_heads, feats):
    rows8 = lax.broadcasted_iota(jnp.int32, (8, feats), 0)
    lanes = lax.broadcasted_iota(jnp.int32, (8, feats), 1)
    if feats == n_heads * HEAD_DIM:
        return jnp.where((lanes // HEAD_DIM) == rows8, jnp.broadcast_to(q_row, (8, feats)), 0.0)
    rows = []
    for h in range(n_heads):
        c, off, g = h // 2, (h % 2) * HEAD_DIM, h // NSA_REP
        r = q_row[:, c * LANES:(c + 1) * LANES]
        rows.append(r if off == g * HEAD_DIM else pltpu.roll(r, HEAD_DIM, axis=1))
    q8 = _stack_rows(rows)
    return jnp.where((lanes // HEAD_DIM) == (rows8 // NSA_REP), q8, 0.0)


def _attend_pages(q8, kts, vts, masks, k_new, v_new, new_ok):
    qb = q8.astype(BF16)
    s = [jnp.where(masks[p], jnp.dot(qb, kts[p].astype(BF16), preferred_element_type=F32), NEG)
         for p in range(len(kts))]
    s_new = jnp.where(new_ok, jnp.sum(q8 * k_new, axis=1, keepdims=True), NEG)
    m = s[0]
    for sp in s[1:]:
        m = jnp.maximum(m, sp)
    m = jnp.maximum(jnp.max(m, axis=1, keepdims=True), s_new)
    e_new = jnp.where(new_ok, jnp.exp(s_new - m), 0.0)
    den = e_new
    acc = e_new * v_new
    for p in range(len(kts)):
        e = jnp.where(masks[p], jnp.exp(s[p] - m), 0.0)
        den = den + jnp.sum(e, axis=1, keepdims=True)
        acc = acc + lax.dot_general(e.astype(BF16), vts[p].astype(BF16), (((1,), (1,)), ((), ())),
                                    preferred_element_type=F32)
    return acc / den


def _decode_kernel(pt_ref, *refs, n_pages, t_pos, w_buf):
    del pt_ref
    cmp_pages = refs[:n_pages]
    sel_pages = refs[n_pages:2 * n_pages]
    moba_pages = refs[2 * n_pages:3 * n_pages]
    (win_ref, q_ref, new_ref, ga_ref, w1_ref, pos_ref, w2_ref, gk_ref,
     oa_ref, ob_ref, xk_ref, xv_ref) = refs[3 * n_pages:]
    past = n_pages * PAGE_SIZE
    n_cmp = past // CMP_STRIDE
    lane = lax.broadcasted_iota(jnp.int32, (1, LANES), 1)
    sub_i = lax.broadcasted_iota(jnp.int32, (LANES, 1), 0)
    row8 = lax.broadcasted_iota(jnp.int32, (8, 1), 0)

    kvw = NSA_KV_WIDTH
    qn8 = _heads_on_rows(q_ref[0, :, 0:NSA_WIDTH], NSA_HEADS, kvw) * SCALE
    qr8 = _heads_on_rows(q_ref[0, :, NSA_WIDTH:2 * NSA_WIDTH], NSA_HEADS, kvw) * SCALE
    qb8 = _heads_on_rows(q_ref[0, :, 2 * NSA_WIDTH:2 * NSA_WIDTH + MOBA_WIDTH], MOBA_HEADS, MOBA_WIDTH)
    new = lambda off, width: new_ref[0, :, off:off + width]
    ks_new, vs_new, kw_new, vw_new = new(0, kvw), new(kvw, kvw), new(2 * kvw, kvw), new(3 * kvw, kvw)
    kb_new, vb_new = new(4 * kvw, MOBA_WIDTH), new(4 * kvw + MOBA_WIDTH, MOBA_WIDTH)
    ga_cols = jnp.broadcast_to(ga_ref[0], (LANES, LANES)).T
    gate = [ga_cols[br * NSA_HEADS:(br + 1) * NSA_HEADS, 0:1] for br in range(3)]
    grp8 = row8 // NSA_REP
    mpage = lambda p, kv: moba_pages[p][kv].reshape(MOBA_WIDTH, PAGE_SIZE).astype(BF16)
    qbb = qb8.astype(BF16)
    raw = [jnp.dot(qbb, mpage(p, 0), preferred_element_type=F32) for p in range(n_pages)]

    x_refs = (xk_ref, xv_ref)
    for p in range(n_pages):
        for kv in range(2):
            x_refs[kv][p * PAGE_SIZE:(p + 1) * PAGE_SIZE, :] = cmp_pages[p][kv].reshape(kvw, PAGE_SIZE).T
    ck_t = _compress_blocks(lambda kv, j: x_refs[kv][pl.ds(j, n_cmp, stride=CMP_STRIDE), :],
                            n_cmp, w1_ref, pos_ref, w2_ref, gk_ref)

    cend_ok = (lane * CMP_STRIDE + (CMP_LEN - 1)) <= t_pos
    cur = t_pos // SEL_BLOCK
    forced = (lane == 0) | (lane >= cur - 1)
    mi = lane * (SEL_BLOCK // CMP_STRIDE)
    amat_t = (((sub_i >= mi) & (sub_i <= mi + 3)).astype(F32)
              + ((sub_i >= mi - 1) & (sub_i <= mi + 2)).astype(F32))
    blocks_per_page = PAGE_SIZE // SEL_BLOCK

    kct, vct = ck_t
    sm = jnp.where(cend_ok, jnp.dot(qn8.astype(BF16), kct.astype(BF16), preferred_element_type=F32), NEG)
    e = jnp.where(cend_ok, jnp.exp(sm - jnp.max(sm, axis=1, keepdims=True)), 0.0)
    den = jnp.sum(e, axis=1, keepdims=True)
    pc = e / jnp.where(den > 0.0, den, 1.0)
    o_c = lax.dot_general(pc.astype(BF16), vct.astype(BF16), (((1,), (1,)), ((), ())), preferred_element_type=F32)
    sel_rows = []
    for g in range(NSA_KV_GROUPS):
        psum = jnp.broadcast_to(jnp.sum(jnp.where(grp8 == g, pc, 0.0), axis=0, keepdims=True), (8, LANES))
        imp = jnp.dot(psum, amat_t, preferred_element_type=F32, precision=lax.Precision.HIGHEST)[0:1, :]
        score = jnp.where(lane > cur, NEG, jnp.where(forced, FORCED, imp))
        rm = jnp.broadcast_to(score, (LANES, LANES))
        cm = rm.T
        rank = jnp.sum(((cm > rm) | ((cm == rm) & (sub_i < lane))).astype(F32), axis=0, keepdims=True)
        sel_rows.append(((rank < SEL_TOPK) & (lane <= cur)).astype(F32))
    sel8 = jnp.where(grp8 == 0, sel_rows[0], sel_rows[1])
    masks = []
    for p in range(n_pages):
        picks = [_lane_col(sel8, p * blocks_per_page + b_, lane) for b_ in range(blocks_per_page)]
        mrow = picks[-1]
        for b_ in range(blocks_per_page - 2, -1, -1):
            mrow = jnp.where(lane < (b_ + 1) * SEL_BLOCK, picks[b_], mrow)
        masks.append(mrow > 0.5)
    page2d = lambda ref, kv: ref[kv].reshape(kvw, ref.shape[-1])
    o_s = _attend_pages(qr8, [page2d(sel_pages[p], 0) for p in range(n_pages)],
                        [page2d(sel_pages[p], 1) for p in range(n_pages)], masks,
                        ks_new, vs_new, _lane_col(sel8, cur, lane) > 0.5)
    wmasks = []
    for c in range(w_buf // LANES):
        rel = t_pos - (past - w_buf + c * LANES + lane)
        wmasks.append((rel >= 0) & (rel < WINDOW))
    win_k, win_v = page2d(win_ref, 0), page2d(win_ref, 1)
    o_w = _attend_pages(qr8, [win_k[:, c * LANES:(c + 1) * LANES] for c in range(w_buf // LANES)],
                        [win_v[:, c * LANES:(c + 1) * LANES] for c in range(w_buf // LANES)], wmasks,
                        kw_new, vw_new, row8 >= 0)
    o8 = gate[0] * o_c + gate[1] * o_s + gate[2] * o_w
    for c in range(NSA_HEADS // 2):
        even, odd = o8[2 * c:2 * c + 1, :], o8[2 * c + 1:2 * c + 2, :]
        if (2 * c) // NSA_REP == 0:
            odd = pltpu.roll(odd, HEAD_DIM, axis=1)
        else:
            even = pltpu.roll(even, HEAD_DIM, axis=1)
        oa_ref[0, :, c * LANES:(c + 1) * LANES] = jnp.where(lane < HEAD_DIM, even, odd)

    pages_per_blk = MOBA_BLOCK // PAGE_SIZE
    cq = t_pos // MOBA_BLOCK
    s_blk = jnp.zeros((MOBA_HEADS, LANES), F32)
    for n_ in range(past // MOBA_BLOCK):
        tot = raw[n_ * pages_per_blk]
        for k_ in range(1, pages_per_blk):
            tot = tot + raw[n_ * pages_per_blk + k_]
        s_blk = jnp.where(lane == n_, jnp.sum(tot, axis=1, keepdims=True) * (1.0 / MOBA_BLOCK), s_blk)
    past_m = lane < cq
    score = jnp.where(past_m, s_blk, NEG)
    rank = jnp.zeros((MOBA_HEADS, LANES), F32)
    for j in range(cq):
        cj = _lane_col(score, j, lane)
        rank = rank + ((cj > score) | ((cj == score) & (lane > j))).astype(F32)
    sel = (past_m & (rank < MOBA_TOPK)).astype(F32)
    picked = [_lane_col(sel, n_, lane) > 0.5 for n_ in range(past // MOBA_BLOCK)]
    s_new = jnp.sum(qb8 * kb_new, axis=1, keepdims=True) * SCALE
    sc = [jnp.where(picked[p // pages_per_blk], raw[p] * SCALE, NEG) for p in range(n_pages)]
    m = sc[0]
    for sp in sc[1:]:
        m = jnp.maximum(m, sp)
    m = jnp.maximum(jnp.max(m, axis=1, keepdims=True), s_new)
    e_new = jnp.exp(s_new - m)
    den = e_new
    acc = e_new * vb_new
    for p in range(n_pages):
        e = jnp.where(picked[p // pages_per_blk], jnp.exp(sc[p] - m), 0.0)
        den = den + jnp.sum(e, axis=1, keepdims=True)
        acc = acc + lax.dot_general(e.astype(BF16), mpage(p, 1), (((1,), (1,)), ((), ())),
                                    preferred_element_type=F32)
    rows8w = lax.broadcasted_iota(jnp.int32, (MOBA_HEADS, MOBA_WIDTH), 0)
    lanes_w = lax.broadcasted_iota(jnp.int32, (MOBA_HEADS, MOBA_WIDTH), 1)
    ob_ref[0] = jnp.sum(jnp.where((lanes_w // HEAD_DIM) == rows8w, acc / den, 0.0), axis=0, keepdims=True)


def _decode(cmp_t, sel_t, moba_t, win_t, layer, page_table, q_rows, new_rows, ga_rows, w1kv, poskv, w2pad, gk):
    n_seq, n_pages = page_table.shape
    past = n_pages * PAGE_SIZE
    w_buf = win_t.shape[-1]
    assert past // CMP_STRIDE == LANES and past % MOBA_BLOCK == 0 and w_buf % LANES == 0

    def page_spec(h, p):
        return pl.BlockSpec((None, None, 2, h, HEAD_DIM, PAGE_SIZE), lambda s, pt: (layer, pt[s, p], 0, 0, 0, 0))

    row_spec = lambda a: pl.BlockSpec((1, 1, a.shape[2]), lambda s, pt: (s, 0, 0))
    full = lambda a: pl.BlockSpec(a.shape, lambda s, pt: (0,) * a.ndim)
    in_specs = ([page_spec(NSA_KV_GROUPS, p) for p in range(n_pages)]
                + [page_spec(NSA_KV_GROUPS, p) for p in range(n_pages)]
                + [page_spec(MOBA_HEADS, p) for p in range(n_pages)]
                + [pl.BlockSpec((None, None, 2, NSA_KV_GROUPS, HEAD_DIM, w_buf), lambda s, pt: (layer, s, 0, 0, 0, 0)),
                   row_spec(q_rows), row_spec(new_rows), row_spec(ga_rows),
                   full(w1kv), full(poskv), full(w2pad), pl.BlockSpec((HEAD_DIM, 1), lambda s, pt: (0, 0))])
    out_spec = pl.BlockSpec((1, 1, NSA_WIDTH), lambda s, pt: (s, 0, 0))
    return pl.pallas_call(
        functools.partial(_decode_kernel, n_pages=n_pages, t_pos=past, w_buf=w_buf),
        grid_spec=pltpu.PrefetchScalarGridSpec(
            num_scalar_prefetch=1, grid=(n_seq,), in_specs=in_specs, out_specs=[out_spec, out_spec],
            scratch_shapes=[pltpu.VMEM((past, NSA_KV_WIDTH), F32)] * 2),
        out_shape=[jax.ShapeDtypeStruct((n_seq, 1, NSA_WIDTH), F32)] * 2,
        compiler_params=_cparams(1),
        name="decode_mixers",
    )(page_table, *([cmp_t] * n_pages), *([sel_t] * n_pages), *([moba_t] * n_pages), win_t,
      q_rows, new_rows, ga_rows, w1kv, poskv, w2pad, gk.reshape(HEAD_DIM, 1))


_SPLITS = (NSA_WIDTH, NSA_KV_WIDTH, NSA_KV_WIDTH, NSA_KV_WIDTH, NSA_KV_WIDTH, NSA_KV_WIDTH, NSA_KV_WIDTH,
           3 * NSA_HEADS, MOBA_WIDTH, MOBA_WIDTH, MOBA_WIDTH, 2 * D_MODEL)
_NAMES = ("qa", "kc", "vc", "ks", "vs", "kw", "vw", "ga", "qb", "kb", "vb", "gm")


def _layer_params(l, w_in, gq_nsa, gk_sel, gk_win, gq_moba, gk_moba, cmp_pos_k, cmp_pos_v,
                  cmp_w1_k, cmp_w2_k, cmp_w1_v, cmp_w2_v):
    wt = w_in[l].T
    off, part = 0, {}
    for name, size in zip(_NAMES, _SPLITS):
        part[name] = wt[off:off + size]
        off += size
    ga_pad = jnp.zeros((HEAD_DIM - 3 * NSA_HEADS, D_MODEL), F32)
    order = ("qa", "qb", "kb", "vb", "ks", "vs", "kw", "vw", "kc", "vc", "ga")
    p = {
        "wt": jnp.concatenate([part[n_] for n_ in order] + [ga_pad, part["gm"]], axis=0).astype(BF16),
        "gains": (gq_nsa[l], gq_moba[l], gk_moba[l], gk_sel[l], gk_win[l]),
    }
    w1kv, poskv, w2pad = [], [], []
    for w1, w2, pos in ((cmp_w1_k[l], cmp_w2_k[l], cmp_pos_k[l]), (cmp_w1_v[l], cmp_w2_v[l], cmp_pos_v[l])):
        w1r = w1.reshape(2, CMP_STRIDE, HEAD_DIM, CMP_HIDDEN)
        lohi = jnp.concatenate([w1r[0], w1r[1]], axis=-1)
        z = jnp.zeros_like(lohi)
        w1kv.append(jnp.stack([jnp.concatenate([lohi, z], axis=-1), jnp.concatenate([z, lohi], axis=-1)], axis=1)
                    .reshape(CMP_STRIDE * NSA_KV_WIDTH, NSA_KV_GROUPS * 2 * CMP_HIDDEN))
        posr = jnp.tile(pos.reshape(2, CMP_STRIDE, 1, HEAD_DIM), (1, 1, NSA_KV_GROUPS, 1))
        poskv.append(jnp.pad(posr.reshape(2, CMP_STRIDE * NSA_KV_WIDTH), ((0, 6), (0, 0))))
        zw = jnp.zeros_like(w2)
        w2pad += [jnp.concatenate([w2, zw], axis=1), jnp.concatenate([zw, w2], axis=1)]
    p["w1kv"] = jnp.stack(w1kv).astype(BF16)
    p["poskv"] = jnp.stack(poskv).astype(BF16)
    p["w2pad"] = jnp.stack(w2pad).astype(BF16)
    return p


def _rope_tables(pos):
    inv = ROPE_THETA ** (-jnp.arange(HALF, dtype=F32) / HALF)
    ang = inv[:, None] * pos.astype(F32)[None, :]
    return jnp.cos(ang), jnp.sin(ang)


def _cache_leaf(feat_major, heads):
    b, _, t = feat_major.shape
    return feat_major.reshape(b, 2, heads, HEAD_DIM, t).transpose(0, 4, 1, 2, 3)


def kernel(x_prompt, x_sample, cache_nsa_cmp, cache_nsa_sel, cache_moba, state_nsa_win, page_table,
           norm_mix, w_in, gq_nsa, gk_cmp, gk_sel, gk_win, gq_moba, gk_moba,
           cmp_pos_k, cmp_pos_v, cmp_w1_k, cmp_w2_k, cmp_w1_v, cmp_w2_v,
           w_up_nsa, w_up_moba, w_out, norm_mlp, w_mlp_up, w_mlp_down):
    depth = w_in.shape[0]
    b, t, d = x_prompt.shape
    n_dec = x_sample.shape[0]
    past_len = page_table.shape[1] * PAGE_SIZE

    cos_p, sin_p = _rope_tables(jnp.arange(t, dtype=jnp.int32))
    cos_s, sin_s = _rope_tables(jnp.full((n_dec,), past_len, dtype=jnp.int32))
    to_t = lambda c: c.transpose(0, 1, 3, 4, 5, 2)
    cmp_t, sel_t, moba_t, win_t = to_t(cache_nsa_cmp), to_t(cache_nsa_sel), to_t(cache_moba), to_t(state_nsa_win)

    xp = x_prompt
    xs = x_sample.reshape(1, n_dec, d)
    leaves = [[] for _ in range(8)]
    tm_p, tm_proj, tm_mlp = 512, 256, 1024
    kvw = NSA_KV_WIDTH
    for l in range(depth):
        p = _layer_params(l, w_in, gq_nsa, gk_sel, gk_win, gq_moba, gk_moba, cmp_pos_k, cmp_pos_v,
                          cmp_w1_k, cmp_w2_k, cmp_w1_v, cmp_w2_v)
        wa, wb, wo = w_up_nsa[l].astype(BF16), w_up_moba[l].astype(BF16), w_out[l].astype(BF16)

        pr = _proj(xp, norm_mix[l], p["wt"], p["gains"], cos_p, sin_p, tm=tm_proj, attention_copies=True)
        kc, vc_t = _compress(pr["c_rows"], p["w1kv"], p["poskv"], p["w2pad"], gk_cmp[l])
        ks_blk, kw_blk = MOBA_WIDTH // kvw, MOBA_WIDTH // kvw + 1
        oa = _nsa(pr["qn_t"], pr["qr_t"], pr["ga_t"], kc, vc_t,
                  pr["k_rows"], ks_blk, pr["v_tt"], ks_blk, pr["k_rows"], kw_blk, pr["v_tt"], kw_blk)
        ob = _moba(pr["qb_t"], pr["k_rows"], 0, pr["v_tt"], 0)
        x2 = _merge(xp.reshape(b * t, d), oa.reshape(b * t, NSA_WIDTH), ob.reshape(b * t, MOBA_WIDTH),
                    pr["gm"], wa, wb, wo, tm=tm_p)
        xp = _mlp(x2, norm_mlp[l], w_mlp_up, w_mlp_down, l, tm=tm_mlp, tf=1024).reshape(b, t, d)
        leaves[0].append(_cache_leaf(pr["cmp_t"], NSA_KV_GROUPS))
        leaves[1].append(_cache_leaf(pr["sel_t"], NSA_KV_GROUPS))
        leaves[2].append(_cache_leaf(pr["moba_t"], MOBA_HEADS))
        leaves[3].append(_cache_leaf(pr["win_t"][:, :, t - min(WINDOW, t):], NSA_KV_GROUPS))

        sr = _proj(xs, norm_mix[l], p["wt"], p["gains"], cos_s, sin_s, tm=n_dec, attention_copies=False)
        seq_rows = lambda a: a[0].T.astype(F32)
        c_new, new_sel, new_win, new_moba = (seq_rows(sr[k_]) for k_ in ("cmp_t", "sel_t", "win_t", "moba_t"))
        q_rows = jnp.concatenate([seq_rows(sr["qn_t"]), seq_rows(sr["qr_t"]), seq_rows(sr["qb_t"])], axis=1)
        new_rows = jnp.concatenate([new_sel, new_win, new_moba], axis=1)
        ga_rows = jnp.pad(seq_rows(sr["ga_t"]), ((0, 0), (0, LANES - HEAD_DIM)))
        oa, ob = _decode(cmp_t, sel_t, moba_t, win_t, l, page_table, q_rows[:, None, :], new_rows[:, None, :],
                         ga_rows[:, None, :], p["w1kv"], p["poskv"], p["w2pad"], gk_cmp[l])
        x2 = _merge(xs.reshape(n_dec, d), oa[:, 0].astype(BF16), ob[:, 0].astype(BF16), sr["gm"], wa, wb, wo, tm=n_dec)
        xs = _mlp(x2, norm_mlp[l], w_mlp_up, w_mlp_down, l, tm=n_dec, tf=1024).reshape(1, n_dec, d)
        seq_leaf = lambda rows, heads: rows.reshape(n_dec, 1, 2, heads, HEAD_DIM)
        leaves[4].append(seq_leaf(c_new, NSA_KV_GROUPS))
        leaves[5].append(seq_leaf(new_sel, NSA_KV_GROUPS))
        leaves[6].append(seq_leaf(new_moba, MOBA_HEADS))
        win_all = jnp.concatenate([state_nsa_win[l], seq_leaf(new_win, NSA_KV_GROUPS)], axis=1)
        leaves[7].append(win_all[:, -min(WINDOW, past_len + 1):])
    return (xp, xs.reshape(n_dec, 1, d)) + tuple(jnp.stack(v) for v in leaves)
```
